```python
import jax, jax.numpy as jnp
from jax import lax
import numpy as np

D_MODEL = 1024
BATCH = 16
SEQ = 2048
DEPTH = 2

GRID_W = 64
CTX_LEN = 256
EPS = 1e-6
ROPE_THETA = 10000.0
Q_BLOCK = 128
N_MOD = 6
N_BRANCH = 3
D_FF = 4 * D_MODEL
A_HEADS = 8
A_KV_HEADS = 2
A_GROUP = A_HEADS // A_KV_HEADS
A_HEAD_DIM = 64
B_HEADS = 4
B_KEY_DIM = 64
B_VAL_DIM = 128
B_GATE_RANK = 16
B_GATE_NORM = 16.0
B_CHUNK = 64
C_HEADS = 8
C_NOPE = 64
C_ROPE = 32
C_VDIM = 64
C_Q_RANK = 384
C_KV_RANK = 256
BRANCH_W = 512
IN_SPLITS = (A_HEADS * A_HEAD_DIM, A_KV_HEADS * A_HEAD_DIM, A_KV_HEADS * A_HEAD_DIM,
             B_HEADS * B_KEY_DIM, B_HEADS * B_KEY_DIM, B_HEADS * B_VAL_DIM, 2 * B_GATE_RANK, B_HEADS * B_VAL_DIM,
             C_Q_RANK, C_KV_RANK, C_ROPE, N_BRANCH * D_MODEL)
D_IN = sum(IN_SPLITS)

kernel_name = 'hybrid_parallel_mixer_dit_prefix'


def rms_norm(x, g):
    xf = x.astype(jnp.float32)
    y = xf * lax.rsqrt(jnp.mean(xf * xf, axis=-1, keepdims=True) + EPS)
    return (y * g.astype(jnp.float32)).astype(x.dtype)


def modulate(h, shift, scale):
    return h * (1.0 + scale) + shift


def axial_rope_tables(rows, rot_dim, dtype):
    row = jnp.repeat(jnp.arange(rows, dtype=jnp.float32), GRID_W)
    col = jnp.tile(jnp.arange(GRID_W, dtype=jnp.float32), rows)
    n_freq = rot_dim // 4
    inv_freq = ROPE_THETA ** (-jnp.arange(n_freq, dtype=jnp.float32) / n_freq)
    ang = jnp.concatenate([row[:, None] * inv_freq, col[:, None] * inv_freq], axis=-1)
    return jnp.cos(ang).astype(dtype), jnp.sin(ang).astype(dtype)


def apply_rope(x, cos, sin):
    xp = x.reshape(x.shape[:-1] + (-1, 2))
    x0, x1 = xp[..., 0], xp[..., 1]
    out = jnp.stack([x0 * cos - x1 * sin, x0 * sin + x1 * cos], axis=-1)
    return out.reshape(x.shape).astype(x.dtype)


def split_in(z):
    idx = np.cumsum(np.array(IN_SPLITS))[:-1].tolist()
    return jnp.split(z, idx, axis=-1)


def block_attention(q, k, v, scale):
    B, KVH, G, N, d = q.shape
    nb = N // Q_BLOCK
    qb = q.reshape(B, KVH, G, nb, Q_BLOCK, d).transpose(3, 0, 1, 2, 4, 5)

    def one_block(qi):
        s = jnp.einsum('bkgqd,bkmd->bkgqm', qi, k).astype(jnp.float32) * scale
        p = jax.nn.softmax(s, axis=-1).astype(v.dtype)
        return jnp.einsum('bkgqm,bkmd->bkgqd', p, v)

    o = lax.map(one_block, qb)
    return o.transpose(1, 2, 3, 0, 4, 5).reshape(B, KVH, G, N, v.shape[-1])


def merge_heads(o):
    B, K, G, N, d = o.shape
    return o.transpose(0, 3, 1, 2, 4).reshape(B, N, K * G * d)


def gqa_mixer(pl, pc, g_q, g_k, cos, sin, need_ctx):
    def heads(q, k, v):
        B, N, _ = q.shape
        q = q.reshape(B, N, A_KV_HEADS, A_GROUP, A_HEAD_DIM).transpose(0, 2, 3, 1, 4)
        k = k.reshape(B, N, A_KV_HEADS, A_HEAD_DIM).transpose(0, 2, 1, 3)
        v = v.reshape(B, N, A_KV_HEADS, A_HEAD_DIM).transpose(0, 2, 1, 3)
        return rms_norm(q, g_q), rms_norm(k, g_k), v

    q_l, k_l, v_l = heads(*pl)
    q_c, k_c, v_c = heads(*pc)
    q_l = apply_rope(q_l, cos, sin)
    k_l = apply_rope(k_l, cos, sin)
    k_all = jnp.concatenate([k_c, k_l], axis=2)
    v_all = jnp.concatenate([v_c, v_l], axis=2)
    scale = A_HEAD_DIM ** -0.5
    o_l = merge_heads(block_attention(q_l, k_all, v_all, scale))
    o_c = merge_heads(block_attention(q_c, k_c, v_c, scale)) if need_ctx else None
    return o_l, o_c


def gla_scan(q, k, v, log_a, s0):
    B, H, N, dk = q.shape
    dv = v.shape[-1]
    nc = N // B_CHUNK
    r = lambda t: t.reshape(B, H, nc, B_CHUNK, t.shape[-1])
    q, k, v, log_a = r(q), r(k), r(v), r(log_a)
    b = jnp.cumsum(log_a, axis=3)
    b_last = b[:, :, :, -1:, :]
    q_dec = q * jnp.exp(b)
    k_dec = k * jnp.exp(-b)
    k_to_end = k * jnp.exp(b_last - b)
    mask = jnp.tril(jnp.ones((B_CHUNK, B_CHUNK), dtype=bool))
    attn = jnp.where(mask, jnp.einsum('bhcid,bhcjd->bhcij', q_dec, k_dec), 0.0)
    o_intra = jnp.einsum('bhcij,bhcjv->bhciv', attn, v)
    u = jnp.einsum('bhcjd,bhcjv->bhcdv', k_to_end, v)
    g = jnp.exp(b_last[:, :, :, 0, :])

    def step(s, inp):
        g_c, u_c = inp
        return g_c[..., None] * s + u_c, s

    s_final, s_enter = lax.scan(step, s0, (g.transpose(2, 0, 1, 3), u.transpose(2, 0, 1, 3, 4)))
    s_enter = s_enter.transpose(1, 2, 0, 3, 4)
    o_inter = jnp.einsum('bhcid,bhcdv->bhciv', q_dec, s_enter)
    return (o_intra + o_inter).reshape(B, H, N, dv), s_final


def gla_bidir(q, k, v, la_f, la_b, s0_f, s0_b):
    o_f, s_f = gla_scan(q, k, v, la_f, s0_f)
    flip = lambda t: jnp.flip(t, axis=2)
    o_b, s_b = gla_scan(flip(q), flip(k), flip(v), flip(la_b), s0_b)
    return o_f + flip(o_b), s_f, s_b


def gla_mixer(pl, pc, w_alpha, b_alpha, g_out):
    def prep(q, k, v, a_lr):
        B, N, _ = q.shape
        heads = lambda t: t.reshape(B, N, B_HEADS, -1).transpose(0, 2, 1, 3).astype(jnp.float32)
        logits = jnp.einsum('bnzr,zrd->zbnd', a_lr.reshape(B, N, 2, B_GATE_RANK), w_alpha) + b_alpha[:, None, None, :]
        log_a = jax.nn.log_sigmoid(logits.astype(jnp.float32)) / B_GATE_NORM
        return heads(q) * (B_KEY_DIM ** -0.5), heads(k), heads(v), heads(log_a[0]), heads(log_a[1])

    q_c, k_c, v_c, lf_c, lb_c = prep(*pc[:4])
    s0 = jnp.zeros((q_c.shape[0], B_HEADS, B_KEY_DIM, B_VAL_DIM), jnp.float32)
    o_c, s_f, s_b = gla_bidir(q_c, k_c, v_c, lf_c, lb_c, s0, s0)
    q_l, k_l, v_l, lf_l, lb_l = prep(*pl[:4])
    o_l, _, _ = gla_bidir(q_l, k_l, v_l, lf_l, lb_l, s_f, s_b)

    def finish(o, gate):
        o = rms_norm(o.transpose(0, 2, 1, 3), g_out)
        B, N = o.shape[:2]
        return (o.reshape(B, N, -1) * jax.nn.silu(gate.astype(jnp.float32))).astype(gate.dtype)

    return finish(o_l, pl[4]), finish(o_c, pc[4])


def mla_mixer(pl, pc, g_cq, g_ckv, w_uq, w_ukv, cos, sin, need_ctx):
    def heads(cq, ckv, kr):
        B, N, _ = cq.shape
        q = (rms_norm(cq, g_cq) @ w_uq).reshape(B, N, C_HEADS, C_NOPE + C_ROPE).transpose(0, 2, 1, 3)
        kv = (rms_norm(ckv, g_ckv) @ w_ukv).reshape(B, N, C_HEADS, C_NOPE + C_VDIM).transpose(0, 2, 1, 3)
        return q, kv[..., :C_NOPE], kv[..., C_NOPE:], kr[:, None]

    def full_key(k_nope, k_rope):
        return jnp.concatenate([k_nope, jnp.broadcast_to(k_rope, k_nope.shape[:-1] + (C_ROPE,))], axis=-1)

    q_l, kn_l, v_l, kr_l = heads(*pl)
    q_c, kn_c, v_c, kr_c = heads(*pc)
    q_l = jnp.concatenate([q_l[..., :C_NOPE], apply_rope(q_l[..., C_NOPE:], cos, sin)], axis=-1)
    k_l = full_key(kn_l, apply_rope(kr_l, cos, sin))
    k_c = full_key(kn_c, kr_c)
    k_all = jnp.concatenate([k_c, k_l], axis=2)
    v_all = jnp.concatenate([v_c, v_l], axis=2)
    scale = (C_NOPE + C_ROPE) ** -0.5
    o_l = merge_heads(block_attention(q_l[:, :, None], k_all, v_all, scale))
    o_c = merge_heads(block_attention(q_c[:, :, None], k_c, v_c, scale)) if need_ctx else None
    return o_l, o_c


def merge_branches(outs, gate_logits, w_branch, w_out):
    B, N, _ = gate_logits.shape
    gates = jax.nn.sigmoid(gate_logits).reshape(B, N, N_BRANCH, D_MODEL)
    branch = jnp.stack(outs, axis=2)
    proj = jnp.einsum('bnzw,zwd->bnzd', branch, w_branch)
    return jnp.sum(gates * proj, axis=2) @ w_out


def mixing_sublayer(h_lat, h_ctx, w_in, a_g_q, a_g_k, b_w_alpha, b_b_alpha, b_g_out,
                    c_g_q, c_g_kv, c_w_uq, c_w_ukv, w_branch, w_out, rope_a, rope_c, need_ctx):
    pl = split_in(h_lat @ w_in)
    pc = split_in(h_ctx @ w_in)
    oa_l, oa_c = gqa_mixer(pl[0:3], pc[0:3], a_g_q, a_g_k, rope_a[0], rope_a[1], need_ctx)
    ob_l, ob_c = gla_mixer(pl[3:8], pc[3:8], b_w_alpha, b_b_alpha, b_g_out)
    oc_l, oc_c = mla_mixer(pl[8:11], pc[8:11], c_g_q, c_g_kv, c_w_uq, c_w_ukv, rope_c[0], rope_c[1], need_ctx)
    y_lat = merge_branches((oa_l, ob_l, oc_l), pl[11], w_branch, w_out)
    y_ctx = merge_branches((oa_c, ob_c, oc_c), pc[11], w_branch, w_out) if need_ctx else None
    return y_lat, y_ctx


def sq_relu_mlp(h, w1, w2):
    return jnp.square(jax.nn.relu(h @ w1)) @ w2


def setup_inputs(seed: int = 0) -> dict:
    key = jax.random.key(seed)
    ks = jax.random.split(key, 24)
    L = DEPTH

    def nrm(k, shape, scale):
        return jax.random.normal(k, shape, jnp.float32) * scale

    def gain(k, shape):
        return 1.0 + 0.05 * jax.random.normal(k, shape, jnp.float32)

    return {
        'x': nrm(ks[0], (BATCH, SEQ, D_MODEL), 1.0),
        'c': nrm(ks[1], (BATCH, D_MODEL), 1.0),
        'ctx': nrm(ks[2], (BATCH, CTX_LEN, D_MODEL), 1.0),
        'c_ctx': nrm(ks[3], (D_MODEL,), 1.0),
        'w_ada': nrm(ks[4], (L, D_MODEL, N_MOD * D_MODEL), 0.5 * D_MODEL ** -0.5),
        'b_ada': nrm(ks[5], (L, N_MOD * D_MODEL), 0.01),
        'g_pre_attn': gain(ks[6], (L, D_MODEL)),
        'g_post_attn': gain(ks[7], (L, D_MODEL)),
        'g_pre_mlp': gain(ks[8], (L, D_MODEL)),
        'g_post_mlp': gain(ks[9], (L, D_MODEL)),
        'w_in': nrm(ks[10], (L, D_MODEL, D_IN), D_MODEL ** -0.5),
        'a_g_q': gain(ks[11], (L, A_HEAD_DIM)),
        'a_g_k': gain(ks[12], (L, A_HEAD_DIM)),
        'b_w_alpha': nrm(ks[13], (L, 2, B_GATE_RANK, B_HEADS * B_KEY_DIM), B_GATE_RANK ** -0.5),
        'b_b_alpha': nrm(ks[14], (L, 2, B_HEADS * B_KEY_DIM), 0.1),
        'b_g_out': gain(ks[15], (L, B_VAL_DIM)),
        'c_g_q': gain(ks[16], (L, C_Q_RANK)),
        'c_g_kv': gain(ks[17], (L, C_KV_RANK)),
        'c_w_uq': nrm(ks[18], (L, C_Q_RANK, C_HEADS * (C_NOPE + C_ROPE)), C_Q_RANK ** -0.5),
        'c_w_ukv': nrm(ks[19], (L, C_KV_RANK, C_HEADS * (C_NOPE + C_VDIM)), C_KV_RANK ** -0.5),
        'w_branch': nrm(ks[20], (L, N_BRANCH, BRANCH_W, D_MODEL), BRANCH_W ** -0.5),
        'w_out': nrm(ks[21], (L, D_MODEL, D_MODEL), D_MODEL ** -0.5),
        'w_mlp_in': nrm(ks[22], (L, D_MODEL, D_FF), D_MODEL ** -0.5),
        'w_mlp_out': nrm(ks[23], (L, D_FF, D_MODEL), D_FF ** -0.5),
    }


def reference(x, c, ctx, c_ctx, w_ada, b_ada, g_pre_attn, g_post_attn, g_pre_mlp, g_post_mlp,
              w_in, a_g_q, a_g_k, b_w_alpha, b_b_alpha, b_g_out, c_g_q, c_g_kv, c_w_uq, c_w_ukv,
              w_branch, w_out, w_mlp_in, w_mlp_out):
    n_lat = x.shape[1]
    rows = n_lat // GRID_W
    rope_a = axial_rope_tables(rows, A_HEAD_DIM, x.dtype)
    rope_c = axial_rope_tables(rows, C_ROPE, x.dtype)
    xc = ctx
    for l in range(DEPTH):
        need_ctx = l < DEPTH - 1
        mod_lat = (jax.nn.silu(c) @ w_ada[l] + b_ada[l])[:, None, :]
        mod_ctx = jax.nn.silu(c_ctx) @ w_ada[l] + b_ada[l]
        sh1, sc1, gt1, sh2, sc2, gt2 = jnp.split(mod_lat, N_MOD, axis=-1)
        csh1, csc1, cgt1, csh2, csc2, cgt2 = jnp.split(mod_ctx, N_MOD, axis=-1)
        h_lat = modulate(rms_norm(x, g_pre_attn[l]), sh1, sc1)
        h_ctx = modulate(rms_norm(xc, g_pre_attn[l]), csh1, csc1)
        y_lat, y_ctx = mixing_sublayer(h_lat, h_ctx, w_in[l], a_g_q[l], a_g_k[l], b_w_alpha[l], b_b_alpha[l],
                                       b_g_out[l], c_g_q[l], c_g_kv[l], c_w_uq[l], c_w_ukv[l],
                                       w_branch[l], w_out[l], rope_a, rope_c, need_ctx)
        x = x + gt1 * rms_norm(y_lat, g_post_attn[l])
        f_lat = sq_relu_mlp(modulate(rms_norm(x, g_pre_mlp[l]), sh2, sc2), w_mlp_in[l], w_mlp_out[l])
        x = x + gt2 * rms_norm(f_lat, g_post_mlp[l])
        if need_ctx:
            xc = xc + cgt1 * rms_norm(y_ctx, g_post_attn[l])
            f_ctx = sq_relu_mlp(modulate(rms_norm(xc, g_pre_mlp[l]), csh2, csc2), w_mlp_in[l], w_mlp_out[l])
            xc = xc + cgt2 * rms_norm(f_ctx, g_post_mlp[l])
    return x
```

```python
import functools

import numpy as np
import jax
import jax.numpy as jnp
from jax import lax
from jax.experimental import pallas as pl
from jax.experimental.pallas import tpu as pltpu

EPS = 1e-6
ROPE_THETA = 10000.0
GRID_W = 64
N_MOD = 6
N_BRANCH = 3
A_HEADS, A_KV_HEADS, A_HEAD_DIM = 8, 2, 64
A_GROUP = A_HEADS // A_KV_HEADS
B_HEADS, B_KEY_DIM, B_VAL_DIM, B_GATE_RANK, B_GATE_NORM, B_CHUNK = 4, 64, 128, 16, 16.0, 64
C_HEADS, C_NOPE, C_ROPE, C_VDIM, C_Q_RANK, C_KV_RANK = 8, 64, 32, 64, 384, 256
BRANCH_W = 512

LANES = 128
V7X_VMEM_BYTES = 64 * 1024 * 1024
VMEM_LIMIT = V7X_VMEM_BYTES - 8 * 1024 * 1024

ROW_TILE = 256
C_HEAD_PAD = LANES

_QA, _KA, _VA = 0, 512, 640
_QB, _KB, _VB, _MISC, _GB = 768, 1024, 1280, 1792, 1920
_CQ, _CKV, _GATES, _W_IN_P = 2432, 2816, 3072, 6144
_MISC_ROPE_LO = 64

F32 = jnp.float32
BF16 = jnp.bfloat16


def _cparams(n_grid_axes):
    return pltpu.CompilerParams(dimension_semantics=("arbitrary",) * n_grid_axes,
                                vmem_limit_bytes=VMEM_LIMIT)


def _const_spec(shape):
    nd = len(shape)
    return pl.BlockSpec(shape, lambda *_: (0,) * nd, pipeline_mode=pl.Buffered(1))


def _rms(x, g):
    return x * lax.rsqrt(jnp.mean(x * x, axis=-1, keepdims=True) + EPS) * g


def _swap_halves(x, half):
    lane = lax.broadcasted_iota(jnp.int32, x.shape, 1)
    first = (lane % (2 * half)) < half
    return jnp.where(first, pltpu.roll(x, LANES - half, 1), pltpu.roll(x, half, 1))


def _rope_slab(y, cos, sin, half):
    return y * cos + _swap_halves(y, half) * sin


def _ada_kernel(c_ref, w_ref, b_ref, o_ref):
    cv = c_ref[...]
    s = (cv / (1.0 + jnp.exp(-cv))).astype(BF16)
    o_ref[0] = jnp.dot(s, w_ref[0].astype(BF16), preferred_element_type=F32) + b_ref[0]


def _ada(cvec, w_ada, b_ada):
    L, D, W = w_ada.shape
    R = cvec.shape[0]
    tn = 1024
    return pl.pallas_call(
        _ada_kernel,
        grid=(L, W // tn),
        in_specs=[pl.BlockSpec((R, D), lambda l, j: (0, 0)),
                  pl.BlockSpec((1, D, tn), lambda l, j: (l, 0, j)),
                  pl.BlockSpec((1, 1, tn), lambda l, j: (l, 0, j))],
        out_specs=pl.BlockSpec((1, R, tn), lambda l, j: (l, 0, j)),
        out_shape=jax.ShapeDtypeStruct((L, R, W), F32),
        compiler_params=_cparams(2),
        name="ada",
    )(cvec, w_ada, b_ada.reshape(L, 1, W))


def _inproj_kernel(x_ref, mod_ref, gpre_ref, w_ref, cosa_ref, sina_ref, cosc_ref, sinc_ref,
                   gqk_ref, bd_ref, gcq_ref, gckv_ref, wuq_ref, wukv_ref,
                   qa_ref, ka_ref, va_ref, qb_ref, kb_ref, vb_ref, misc_ref, gb_ref,
                   qc_ref, kc_ref, vc_ref, gates_ref):
    x = x_ref[0]
    m = mod_ref[0, 0]
    h = (_rms(x, gpre_ref[...]) * (1.0 + m[1:2]) + m[0:1]).astype(BF16)

    def proj(lo, hi):
        return jnp.dot(h, w_ref[:, lo:hi], preferred_element_type=F32)

    za = proj(_QA, _QB)
    cosa, sina = cosa_ref[...], sina_ref[...]

    def headnorm_rope(z, gain, out_ref, out_lo):
        w = z.shape[1]
        sq = z * z
        sq_hi = sq.astype(BF16)
        sq_lo = (sq - sq_hi.astype(F32)).astype(BF16)
        bd = bd_ref[:w, :w]
        ss = (jnp.dot(sq_hi, bd, preferred_element_type=F32)
              + jnp.dot(sq_lo, bd, preferred_element_type=F32))
        y = z * lax.rsqrt(ss * (1.0 / A_HEAD_DIM) + EPS) * gain
        for s in range(w // LANES):
            ys = _rope_slab(y[:, s * LANES:(s + 1) * LANES], cosa, sina, A_HEAD_DIM // 2)
            out_ref[0, :, out_lo + s * LANES:out_lo + (s + 1) * LANES] = ys.astype(out_ref.dtype)

    headnorm_rope(za[:, 0:256], gqk_ref[:, 0:256], qa_ref, 0)
    headnorm_rope(za[:, 256:512], gqk_ref[:, 256:512], qa_ref, 256)
    headnorm_rope(za[:, 512:640], gqk_ref[:, 512:640], ka_ref, 0)
    va_ref[0] = za[:, 640:768].astype(BF16)

    zb = proj(_QB, _CQ)
    qb_ref[0] = zb[:, 0:256] * (B_KEY_DIM ** -0.5)
    kb_ref[0] = zb[:, 256:512]
    vb_ref[0] = zb[:, 512:1024].astype(BF16)
    cosc, sinc = cosc_ref[...], sinc_ref[...]
    misc = _rope_slab(zb[:, 1024:1152], cosc, sinc, C_ROPE // 2)
    misc_ref[0] = misc.astype(BF16)
    gb_ref[0] = zb[:, 1152:1664]

    zc = proj(_CQ, _GATES)
    cqn = _rms(zc[:, 0:C_Q_RANK], gcq_ref[...]).astype(BF16)
    qup = jnp.dot(cqn, wuq_ref[...], preferred_element_type=F32)
    scale_c = (C_NOPE + C_ROPE) ** -0.5
    for hh in range(C_HEADS):
        sl = slice(hh * C_HEAD_PAD, (hh + 1) * C_HEAD_PAD)
        qc_ref[0, :, sl] = (_rope_slab(qup[:, sl], cosc, sinc, C_ROPE // 2) * scale_c).astype(BF16)
    ckvn = _rms(zc[:, C_Q_RANK:C_Q_RANK + C_KV_RANK], gckv_ref[...]).astype(BF16)
    kvup = jnp.dot(ckvn, wukv_ref[...], preferred_element_type=F32)
    lane = lax.broadcasted_iota(jnp.int32, misc.shape, 1)
    krope = jnp.where((lane >= _MISC_ROPE_LO) & (lane < _MISC_ROPE_LO + C_ROPE), misc, 0.0)
    for hh in range(C_HEADS):
        sl = slice(hh * C_HEAD_PAD, (hh + 1) * C_HEAD_PAD)
        kc_ref[0, :, sl] = (kvup[:, sl] + krope).astype(BF16)
    vc_ref[0] = kvup[:, C_HEADS * C_HEAD_PAD:].astype(BF16)

    zg = proj(_GATES, _W_IN_P)
    gates_ref[0] = (1.0 / (1.0 + jnp.exp(-zg))).astype(BF16)


def _inproj(xs, modsel, gpre, w_in_p, tabs, gqk, bd, gcq, gckv, wuq, wukv, n_ctx):
    B, N, D = xs.shape
    tr = ROW_TILE
    nct = n_ctx // tr
    row = lambda w: pl.BlockSpec((1, tr, w), lambda b, t: (b, t, 0))
    tab = pl.BlockSpec((tr, LANES), lambda b, t: (t, 0))
    outs = [("qa", 512, BF16), ("ka", 128, BF16), ("va", 128, BF16),
            ("qb", 256, F32), ("kb", 256, F32), ("vb", 512, BF16), ("misc", 128, BF16), ("gb", 512, F32),
            ("qc", C_HEADS * C_HEAD_PAD, BF16), ("kc", C_HEADS * C_HEAD_PAD, BF16),
            ("vc", C_HEADS * C_VDIM, BF16), ("gates", N_BRANCH * D, BF16)]
    return pl.pallas_call(
        _inproj_kernel,
        grid=(B, N // tr),
        in_specs=[row(D),
                  pl.BlockSpec((1, 1, N_MOD, D), lambda b, t: (b, jnp.where(t < nct, 0, 1), 0, 0)),
                  _const_spec(gpre.shape), _const_spec(w_in_p.shape),
                  tab, tab, tab, tab,
                  _const_spec(gqk.shape), _const_spec(bd.shape), _const_spec(gcq.shape),
                  _const_spec(gckv.shape), _const_spec(wuq.shape), _const_spec(wukv.shape)],
        out_specs=[row(w) for _, w, _ in outs],
        out_shape=[jax.ShapeDtypeStruct((B, N, w), dt) for _, w, dt in outs],
        compiler_params=_cparams(2),
        name="inproj",
    )(xs, modsel, gpre, w_in_p, *tabs, gqk, bd, gcq, gckv, wuq, wukv)


def _attn_kernel(q_ref, k_ref, v_ref, o_ref, *, n_heads, q_w, kv_group, v_w, n_ctx, t0, nct):
    def body(n_keys):
        for hh in range(n_heads):
            kvh = hh // kv_group
            q = q_ref[0, :, hh * q_w:(hh + 1) * q_w]
            k = k_ref[0, 0:n_keys, kvh * q_w:(kvh + 1) * q_w]
            s = lax.dot_general(q, k, (((1,), (1,)), ((), ())), preferred_element_type=F32)
            p = jnp.exp(s - jnp.max(s, axis=-1, keepdims=True))
            l = jnp.sum(p, axis=-1, keepdims=True)
            v = v_ref[0, 0:n_keys, kvh * v_w:(kvh + 1) * v_w]
            o = jnp.dot(p.astype(BF16), v, preferred_element_type=F32)
            o_ref[0, :, hh * v_w:(hh + 1) * v_w] = (o / l).astype(o_ref.dtype)

    if t0 < nct:
        t = pl.program_id(1) + t0
        pl.when(t < nct)(lambda: body(n_ctx))
        pl.when(t >= nct)(lambda: body(k_ref.shape[1]))
    else:
        body(k_ref.shape[1])


def _attention(q, k, v, *, n_heads, q_w, kv_group, v_w, n_ctx, need_ctx, name):
    B, N, _ = q.shape
    tq = ROW_TILE
    nct = n_ctx // tq
    t0 = 0 if need_ctx else nct
    kern = functools.partial(_attn_kernel, n_heads=n_heads, q_w=q_w, kv_group=kv_group, v_w=v_w,
                             n_ctx=n_ctx, t0=t0, nct=nct)
    return pl.pallas_call(
        kern,
        grid=(B, N // tq - t0),
        in_specs=[pl.BlockSpec((1, tq, q.shape[2]), lambda b, t: (b, t + t0, 0)),
                  pl.BlockSpec((1, N, k.shape[2]), lambda b, t: (b, 0, 0)),
                  pl.BlockSpec((1, N, v.shape[2]), lambda b, t: (b, 0, 0))],
        out_specs=pl.BlockSpec((1, tq, n_heads * v_w), lambda b, t: (b, t + t0, 0)),
        out_shape=jax.ShapeDtypeStruct((B, N, n_heads * v_w), BF16),
        compiler_params=_cparams(2),
        name=name,
    )(q, k, v)


def _gla_kernel(q_ref, k_ref, v_ref, misc_ref, gate_ref, wal_ref, bal_ref, gout_ref, o_ref,
                la_scr, acc_scr, s_scr, *, n_ctx):
    n_rows = q_ref.shape[1]
    ck = B_CHUNK
    nc, ncc = n_rows // ck, n_ctx // ck
    kw = B_HEADS * B_KEY_DIM

    blk = 256
    def decay_rows(i, carry):
        r0 = pl.multiple_of(i * blk, blk)
        z = jnp.dot(misc_ref[0, pl.ds(r0, blk), :], wal_ref[...], preferred_element_type=F32) + bal_ref[...]
        la_scr[pl.ds(r0, blk), :] = (jnp.minimum(z, 0.0) - jnp.log1p(jnp.exp(-jnp.abs(z)))) * (1.0 / B_GATE_NORM)
        return carry
    lax.fori_loop(0, n_rows // blk, decay_rows, 0)

    ri = lax.broadcasted_iota(jnp.int32, (ck, ck), 0)
    ci = lax.broadcasted_iota(jnp.int32, (ck, ck), 1)
    eye = ri == ci

    def chunk_step(c, backward):
        r0 = pl.multiple_of(c * ck, ck)
        rows = pl.ds(r0, ck)
        keep = (ci >= ri) if backward else (ci <= ri)
        la = la_scr[rows, kw:2 * kw] if backward else la_scr[rows, 0:kw]
        b = jnp.dot(keep.astype(F32), la, precision=lax.Precision.HIGHEST, preferred_element_type=F32)
        bl = b[0:1, :] if backward else b[ck - 1:ck, :]
        q = q_ref[0, rows, :] * jnp.exp(b)
        kd = k_ref[0, rows, :] * jnp.exp(-b)
        ke = k_ref[0, rows, :] * jnp.exp(bl - b)
        g = jnp.exp(bl)
        v = v_ref[0, rows, :]
        for hh in range(B_HEADS):
            ks = slice(hh * B_KEY_DIM, (hh + 1) * B_KEY_DIM)
            vs = slice(hh * B_VAL_DIM, (hh + 1) * B_VAL_DIM)
            qh, kdh, keh, vh = q[:, ks].astype(BF16), kd[:, ks].astype(BF16), ke[:, ks].astype(BF16), v[:, vs]
            attn = lax.dot_general(qh, kdh, (((1,), (1,)), ((), ())), preferred_element_type=F32)
            attn = jnp.where(keep, attn, 0.0).astype(BF16)
            s_in = s_scr[hh]
            o = (jnp.dot(attn, vh, preferred_element_type=F32)
                 + jnp.dot(qh, s_in.astype(BF16), preferred_element_type=F32))
            u = lax.dot_general(keh, vh, (((0,), (0,)), ((), ())), preferred_element_type=F32)
            g_col = jnp.sum(jnp.where(eye, jnp.broadcast_to(g[:, ks], (ck, ck)), 0.0), axis=1, keepdims=True)
            s_scr[hh] = g_col * s_in + u
            if not backward:
                acc_scr[rows, vs] = o
            else:
                tot = _rms(acc_scr[rows, vs] + o, gout_ref[...])
                gt = gate_ref[0, rows, vs]
                o_ref[0, rows, vs] = (tot * (gt / (1.0 + jnp.exp(-gt)))).astype(o_ref.dtype)

    s_scr[...] = jnp.zeros_like(s_scr)
    def fwd(i, carry):
        chunk_step(i, False)
        return carry
    lax.fori_loop(0, nc, fwd, 0)

    s_scr[...] = jnp.zeros_like(s_scr)
    def bwd(i, carry):
        c = jnp.where(i < ncc, ncc - 1 - i, nc - 1 - (i - ncc))
        chunk_step(c, True)
        return carry
    lax.fori_loop(0, nc, bwd, 0)


def _gla(qb, kb, vb, misc, gb, wal, bal, gout, n_ctx):
    B, N, _ = qb.shape
    vw = B_HEADS * B_VAL_DIM
    blk = lambda w: pl.BlockSpec((1, N, w), lambda b: (b, 0, 0))
    return pl.pallas_call(
        functools.partial(_gla_kernel, n_ctx=n_ctx),
        grid=(B,),
        in_specs=[blk(qb.shape[2]), blk(kb.shape[2]), blk(vw), blk(LANES), blk(vw),
                  _const_spec(wal.shape), _const_spec(bal.shape), _const_spec(gout.shape)],
        out_specs=blk(vw),
        out_shape=jax.ShapeDtypeStruct((B, N, vw), BF16),
        scratch_shapes=[pltpu.VMEM((N, 2 * B_HEADS * B_KEY_DIM), F32),
                        pltpu.VMEM((N, vw), F32),
                        pltpu.VMEM((B_HEADS, B_KEY_DIM, B_VAL_DIM), F32)],
        compiler_params=_cparams(1),
        name="gla",
    )(qb, kb, vb, misc, gb, wal, bal, gout)


def _merge_kernel(x_ref, mod_ref, oa_ref, ob_ref, oc_ref, gates_ref, wbr_ref, wout_ref,
                  gpost_ref, gpre2_ref, x1_ref, h2_ref):
    d = x_ref.shape[2]
    m = mod_ref[0, 0]
    acc = None
    for z, o_ref in enumerate((oa_ref, ob_ref, oc_ref)):
        proj = jnp.dot(o_ref[0], wbr_ref[z], preferred_element_type=F32)
        term = gates_ref[0, :, z * d:(z + 1) * d].astype(F32) * proj
        acc = term if acc is None else acc + term
    y = jnp.dot(acc.astype(BF16), wout_ref[...], preferred_element_type=F32)
    x1 = x_ref[0] + m[2:3] * _rms(y, gpost_ref[...])
    x1_ref[0] = x1
    h2_ref[0] = (_rms(x1, gpre2_ref[...]) * (1.0 + m[4:5]) + m[3:4]).astype(BF16)


def _mlp_kernel(x1_ref, h2_ref, mod_ref, w1_ref, w2_ref, gpost_ref, o_ref):
    m = mod_ref[0, 0]
    u = jnp.dot(h2_ref[0], w1_ref[...], preferred_element_type=F32)
    a = jnp.square(jnp.maximum(u, 0.0)).astype(BF16)
    f = jnp.dot(a, w2_ref[...], preferred_element_type=F32)
    o_ref[0] = x1_ref[0] + m[5:6] * _rms(f, gpost_ref[...])


def _post(xs, modsel, oa, ob, oc, gates, wbr, wout, gpost, gpre2, w1, w2, gpost2, n_ctx, need_ctx):
    B, N, D = xs.shape
    tr = ROW_TILE
    nct = n_ctx // tr
    t0 = 0 if need_ctx else nct
    row = lambda w: pl.BlockSpec((1, tr, w), lambda b, t: (b, t + t0, 0))
    mod = pl.BlockSpec((1, 1, N_MOD, D), lambda b, t: (b, jnp.where(t + t0 < nct, 0, 1), 0, 0))
    grid = (B, N // tr - t0)
    x1, h2 = pl.pallas_call(
        _merge_kernel,
        grid=grid,
        in_specs=[row(D), mod, row(BRANCH_W), row(BRANCH_W), row(BRANCH_W), row(N_BRANCH * D),
                  _const_spec(wbr.shape), _const_spec(wout.shape), _const_spec(gpost.shape),
                  _const_spec(gpre2.shape)],
        out_specs=[row(D), row(D)],
        out_shape=[jax.ShapeDtypeStruct((B, N, D), F32), jax.ShapeDtypeStruct((B, N, D), BF16)],
        compiler_params=_cparams(2),
        name="merge",
    )(xs, modsel, oa, ob, oc, gates, wbr, wout, gpost, gpre2)
    n_out = N - t0 * tr
    return pl.pallas_call(
        _mlp_kernel,
        grid=grid,
        in_specs=[row(D), row(D), mod, _const_spec(w1.shape), _const_spec(w2.shape),
                  _const_spec(gpost2.shape)],
        out_specs=pl.BlockSpec((1, tr, D), lambda b, t: (b, t, 0)),
        out_shape=jax.ShapeDtypeStruct((B, n_out, D), F32),
        compiler_params=_cparams(2),
        name="mlp",
    )(x1, h2, modsel, w1, w2, gpost2)


def _pairs_split(n):
    return np.concatenate([np.arange(0, n, 2), np.arange(1, n, 2)])


def _in_perm():
    perm = np.full((_W_IN_P,), -1, np.int64)
    half = _pairs_split(A_HEAD_DIM)
    for hh in range(A_HEADS):
        perm[_QA + hh * A_HEAD_DIM:_QA + (hh + 1) * A_HEAD_DIM] = hh * A_HEAD_DIM + half
    for hh in range(A_KV_HEADS):
        perm[_KA + hh * A_HEAD_DIM:_KA + (hh + 1) * A_HEAD_DIM] = 512 + hh * A_HEAD_DIM + half
    perm[_VA:_VA + 128] = 640 + np.arange(128)
    perm[_QB:_QB + 256] = 768 + np.arange(256)
    perm[_KB:_KB + 256] = 1024 + np.arange(256)
    perm[_VB:_VB + 512] = 1280 + np.arange(512)
    perm[_MISC:_MISC + 2 * B_GATE_RANK] = 1792 + np.arange(2 * B_GATE_RANK)
    perm[_MISC + _MISC_ROPE_LO:_MISC + _MISC_ROPE_LO + C_ROPE] = 2976 + _pairs_split(C_ROPE)
    perm[_GB:_GB + 512] = 1824 + np.arange(512)
    perm[_CQ:_CQ + C_Q_RANK] = 2336 + np.arange(C_Q_RANK)
    perm[_CKV:_CKV + C_KV_RANK] = 2720 + np.arange(C_KV_RANK)
    perm[_GATES:] = 3008 + np.arange(_W_IN_P - _GATES)
    return perm


def _take_cols(w, perm):
    cols = jnp.take(w, jnp.asarray(np.maximum(perm, 0)), axis=-1)
    return jnp.where(jnp.asarray(perm >= 0), cols, 0.0)


def _uq_perm():
    perm = np.full((C_HEADS * C_HEAD_PAD,), -1, np.int64)
    w = C_NOPE + C_ROPE
    for hh in range(C_HEADS):
        perm[hh * C_HEAD_PAD:hh * C_HEAD_PAD + C_NOPE] = hh * w + np.arange(C_NOPE)
        perm[hh * C_HEAD_PAD + C_NOPE:hh * C_HEAD_PAD + w] = hh * w + C_NOPE + _pairs_split(C_ROPE)
    return perm


def _ukv_perm():
    perm = np.full((C_HEADS * C_HEAD_PAD + C_HEADS * C_VDIM,), -1, np.int64)
    w = C_NOPE + C_VDIM
    for hh in range(C_HEADS):
        perm[hh * C_HEAD_PAD:hh * C_HEAD_PAD + C_NOPE] = hh * w + np.arange(C_NOPE)
        perm[C_HEADS * C_HEAD_PAD + hh * C_VDIM:C_HEADS * C_HEAD_PAD + (hh + 1) * C_VDIM] = (
            hh * w + C_NOPE + np.arange(C_VDIM))
    return perm


def _rope_angles(n_lat, rot_dim):
    rows = n_lat // GRID_W
    row = jnp.repeat(jnp.arange(rows, dtype=F32), GRID_W)
    col = jnp.tile(jnp.arange(GRID_W, dtype=F32), rows)
    n_freq = rot_dim // 4
    inv_freq = ROPE_THETA ** (-jnp.arange(n_freq, dtype=F32) / n_freq)
    ang = jnp.concatenate([row[:, None] * inv_freq, col[:, None] * inv_freq], axis=-1)
    return jnp.cos(ang), jnp.sin(ang)


def _rope_tables(n_ctx, n_lat):
    cos, sin = _rope_angles(n_lat, A_HEAD_DIM)
    cos_a = jnp.concatenate([cos] * 4, axis=-1)
    sin_a = jnp.concatenate([-sin, sin, -sin, sin], axis=-1)
    cos, sin = _rope_angles(n_lat, C_ROPE)
    one, zero = jnp.ones((n_lat, 1), F32), jnp.zeros((n_lat, 1), F32)
    lo, hi = _MISC_ROPE_LO, LANES - _MISC_ROPE_LO - C_ROPE
    cos_c = jnp.concatenate([jnp.tile(one, (1, lo)), cos, cos, jnp.tile(one, (1, hi))], axis=-1)
    sin_c = jnp.concatenate([jnp.tile(zero, (1, lo)), -sin, sin, jnp.tile(zero, (1, hi))], axis=-1)
    ident = lambda t, v: jnp.concatenate([jnp.full((n_ctx, LANES), v, F32), t], axis=0)
    return ident(cos_a, 1.0), ident(sin_a, 0.0), ident(cos_c, 1.0), ident(sin_c, 0.0)


def kernel(x, c, ctx, c_ctx, w_ada, b_ada, g_pre_attn, g_post_attn, g_pre_mlp, g_post_mlp, w_in, a_g_q, a_g_k,
           b_w_alpha, b_b_alpha, b_g_out, c_g_q, c_g_kv, c_w_uq, c_w_ukv, w_branch, w_out, w_mlp_in, w_mlp_out):
    B, n_lat, D = x.shape
    n_ctx = ctx.shape[1]
    L = w_ada.shape[0]
    assert n_ctx % ROW_TILE == 0 and n_lat % ROW_TILE == 0 and n_ctx % B_CHUNK == 0

    n_vec = -(-(B + 1) // 8) * 8
    cvec = jnp.concatenate([c, c_ctx[None], jnp.zeros((n_vec - B - 1, D), F32)], axis=0)
    mods = _ada(cvec, w_ada, b_ada)
    mod_lat = mods[:, :B].reshape(L, B, 1, N_MOD, D)
    mod_ctx = jnp.broadcast_to(mods[:, B].reshape(L, 1, 1, N_MOD, D), (L, B, 1, N_MOD, D))
    modsel = jnp.concatenate([mod_ctx, mod_lat], axis=2)

    tabs = _rope_tables(n_ctx, n_lat)
    w_in_p = _take_cols(w_in, _in_perm()).astype(BF16)
    half = _pairs_split(A_HEAD_DIM)
    gqk = jnp.concatenate([jnp.tile(a_g_q[:, half] * (A_HEAD_DIM ** -0.5), (1, A_HEADS)),
                           jnp.tile(a_g_k[:, half], (1, A_KV_HEADS))], axis=-1)
    bd = jnp.asarray(np.kron(np.eye(256 // A_HEAD_DIM), np.ones((A_HEAD_DIM, A_HEAD_DIM))), BF16)
    wuq = _take_cols(c_w_uq, _uq_perm()).astype(BF16)
    wukv = _take_cols(c_w_ukv, _ukv_perm()).astype(BF16)
    kw = B_HEADS * B_KEY_DIM
    wal = jnp.zeros((L, LANES, 2 * kw), F32)
    wal = wal.at[:, 0:B_GATE_RANK, 0:kw].set(b_w_alpha[:, 0])
    wal = wal.at[:, B_GATE_RANK:2 * B_GATE_RANK, kw:].set(b_w_alpha[:, 1]).astype(BF16)
    bal = b_b_alpha.reshape(L, 1, 2 * kw)
    wbr, wout = w_branch.astype(BF16), w_out.astype(BF16)
    w1, w2 = w_mlp_in.astype(BF16), w_mlp_out.astype(BF16)
    vec = lambda g, l: g[l][None, :]

    xs = jnp.concatenate([ctx, x], axis=1)
    for l in range(L):
        need_ctx = l < L - 1
        qa, ka, va, qb, kb, vb, misc, gb, qc, kc, vc, gates = _inproj(
            xs, modsel[l], vec(g_pre_attn, l), w_in_p[l], tabs, vec(gqk, l), bd,
            vec(c_g_q, l), vec(c_g_kv, l), wuq[l], wukv[l], n_ctx)
        oa = _attention(qa, ka, va, n_heads=A_HEADS, q_w=A_HEAD_DIM, kv_group=A_GROUP, v_w=A_HEAD_DIM,
                        n_ctx=n_ctx, need_ctx=need_ctx, name="attn_a")
        oc = _attention(qc, kc, vc, n_heads=C_HEADS, q_w=C_HEAD_PAD, kv_group=1, v_w=C_VDIM,
                        n_ctx=n_ctx, need_ctx=need_ctx, name="attn_c")
        ob = _gla(qb, kb, vb, misc, gb, wal[l], bal[l], vec(b_g_out, l), n_ctx)
        xs = _post(xs, modsel[l], oa, ob, oc, gates, wbr[l], wout[l], vec(g_post_attn, l),
                   vec(g_pre_mlp, l), w1[l], w2[l], vec(g_post_mlp, l), n_ctx, need_ctx)
    return xs
```

```python
import functools

import jax
import jax.numpy as jnp
from jax import lax
from jax.experimental import pallas as pl
from jax.experimental.pallas import tpu as pltpu

EPS = 1e-6
ROPE_THETA = 10000.0
GRID_W = 64
N_MOD = 6
N_BRANCH = 3
A_HEADS, A_KV_HEADS, A_HEAD_DIM = 8, 2, 64
A_GROUP = A_HEADS // A_KV_HEADS
B_HEADS, B_KEY_DIM, B_VAL_DIM, B_GATE_RANK, B_GATE_NORM, B_CHUNK = 4, 64, 128, 16, 16.0, 64
C_HEADS, C_NOPE, C_ROPE, C_VDIM, C_Q_RANK, C_KV_RANK = 8, 64, 32, 64, 384, 256
BRANCH_W = 512

LANES = 128
V7X_VMEM_BYTES = 64 * 1024 * 1024
VMEM_LIMIT = V7X_VMEM_BYTES - 8 * 1024 * 1024

ROW_TILE = 256
GLA_BLOCK = 256
C_HEAD_PAD = LANES

_QA, _KA, _VA = 0, 512, 640
_QB, _KB, _VB, _MISC, _GB = 768, 1024, 1280, 1792, 1920
_CQ, _CKV, _GATES, _W_IN_P = 2432, 2816, 3072, 6144
_MISC_ROPE_LO = 64

F32 = jnp.float32
BF16 = jnp.bfloat16
LOG2E = 1.4426950408889634


def _cparams(n_grid_axes):
    return pltpu.CompilerParams(dimension_semantics=("arbitrary",) * n_grid_axes,
                                vmem_limit_bytes=VMEM_LIMIT)


def _const_spec(shape):
    nd = len(shape)
    return pl.BlockSpec(shape, lambda *_: (0,) * nd, pipeline_mode=pl.Buffered(1))


def _stream_specs(tr, d, nct, dual, t0=0):
    if not dual:
        return [pl.BlockSpec((1, tr, d), lambda b, t: (b, t + t0, 0))]
    return [pl.BlockSpec((1, tr, d), lambda b, t: (b, jnp.minimum(t + t0, nct - 1), 0)),
            pl.BlockSpec((1, tr, d), lambda b, t: (b, jnp.maximum(t + t0 - nct, 0), 0))]


def _stream_tile(refs, nct, t0=0):
    if len(refs) == 1:
        return refs[0][0]
    is_ctx = pl.program_id(1) + t0 < nct
    return jnp.where(is_ctx, refs[0][0], refs[1][0])


def _rms(x, g):
    return x * lax.rsqrt(jnp.mean(x * x, axis=-1, keepdims=True) + EPS) * g


def _swap_halves(x, half):
    lane = lax.broadcasted_iota(jnp.int32, x.shape, 1)
    first = (lane % (2 * half)) < half
    return jnp.where(first, pltpu.roll(x, LANES - half, 1), pltpu.roll(x, half, 1))


def _rope_slab(y, cos, sin, half):
    return y * cos + _swap_halves(y, half) * sin


def _ada_kernel(c_ref, w_ref, b_ref, o_ref):
    cv = c_ref[...]
    s = (cv / (1.0 + jnp.exp(-cv))).astype(BF16)
    o_ref[0] = jnp.dot(s, w_ref[0].astype(BF16), preferred_element_type=F32) + b_ref[0]


def _ada(cvec, w_ada, b_ada):
    L, D, W = w_ada.shape
    R = cvec.shape[0]
    tn = 1024
    return pl.pallas_call(
        _ada_kernel,
        grid=(L, W // tn),
        in_specs=[pl.BlockSpec((R, D), lambda l, j: (0, 0)),
                  pl.BlockSpec((1, D, tn), lambda l, j: (l, 0, j)),
                  pl.BlockSpec((1, 1, tn), lambda l, j: (l, 0, j))],
        out_specs=pl.BlockSpec((1, R, tn), lambda l, j: (l, 0, j)),
        out_shape=jax.ShapeDtypeStruct((L, R, W), F32),
        compiler_params=_cparams(2),
        name="ada",
    )(cvec, w_ada, b_ada.reshape(L, 1, W))


def _inproj_kernel(*refs, n_stream, nct):
    x_refs = refs[:n_stream]
    (mod_ref, gpre_ref, w_ref, cosa_ref, sina_ref, cosc_ref, sinc_ref,
     gqk_ref, bd_ref, gcq_ref, gckv_ref, wuq_ref, wukv_ref,
     qa_ref, ka_ref, va_ref, qb_ref, kb_ref, vb_ref, misc_ref, gb_ref,
     qc_ref, kc_ref, vc_ref, gates_ref) = refs[n_stream:]
    x = _stream_tile(x_refs, nct)
    m = mod_ref[0, 0]
    h = (_rms(x, gpre_ref[...]) * (1.0 + m[1:2]) + m[0:1]).astype(BF16)

    def proj(lo, hi):
        return jnp.dot(h, w_ref[:, lo:hi], preferred_element_type=F32)

    za = proj(_QA, _QB)
    cosa, sina = cosa_ref[...], sina_ref[...]

    def headnorm_rope(z, gain, out_ref, out_lo):
        w = z.shape[1]
        sq = z * z
        sq_hi = sq.astype(BF16)
        sq_lo = (sq - sq_hi.astype(F32)).astype(BF16)
        bd = bd_ref[:w, :w]
        ss = (jnp.dot(sq_hi, bd, preferred_element_type=F32)
              + jnp.dot(sq_lo, bd, preferred_element_type=F32))
        y = z * lax.rsqrt(ss * (1.0 / A_HEAD_DIM) + EPS) * gain
        for s in range(w // LANES):
            ys = _rope_slab(y[:, s * LANES:(s + 1) * LANES], cosa, sina, A_HEAD_DIM // 2)
            out_ref[0, :, out_lo + s * LANES:out_lo + (s + 1) * LANES] = ys.astype(out_ref.dtype)

    headnorm_rope(za[:, 0:256], gqk_ref[:, 0:256], qa_ref, 0)
    headnorm_rope(za[:, 256:512], gqk_ref[:, 256:512], qa_ref, 256)
    headnorm_rope(za[:, 512:640], gqk_ref[:, 512:640], ka_ref, 0)
    va_ref[0] = za[:, 640:768].astype(BF16)

    zb = proj(_QB, _CQ)
    qb_ref[0] = zb[:, 0:256] * (B_KEY_DIM ** -0.5)
    kb_ref[0] = zb[:, 256:512]
    vb_ref[0] = zb[:, 512:1024].astype(BF16)
    cosc, sinc = cosc_ref[...], sinc_ref[...]
    misc = _rope_slab(zb[:, 1024:1152], cosc, sinc, C_ROPE // 2)
    misc_ref[0] = misc.astype(BF16)
    gb_ref[0] = zb[:, 1152:1664]

    zc = proj(_CQ, _GATES)
    cqn = _rms(zc[:, 0:C_Q_RANK], gcq_ref[...]).astype(BF16)
    qup = jnp.dot(cqn, wuq_ref[...], preferred_element_type=F32)
    scale_c = (C_NOPE + C_ROPE) ** -0.5 * LOG2E
    for hh in range(C_HEADS):
        sl = slice(hh * C_HEAD_PAD, (hh + 1) * C_HEAD_PAD)
        qc_ref[0, :, sl] = (_rope_slab(qup[:, sl], cosc, sinc, C_ROPE // 2) * scale_c).astype(BF16)
    ckvn = _rms(zc[:, C_Q_RANK:C_Q_RANK + C_KV_RANK], gckv_ref[...]).astype(BF16)
    kvup = jnp.dot(ckvn, wukv_ref[...], preferred_element_type=F32)
    lane = lax.broadcasted_iota(jnp.int32, misc.shape, 1)
    krope = jnp.where((lane >= _MISC_ROPE_LO) & (lane < _MISC_ROPE_LO + C_ROPE), misc, 0.0)
    for hh in range(C_HEADS):
        sl = slice(hh * C_HEAD_PAD, (hh + 1) * C_HEAD_PAD)
        kc_ref[0, :, sl] = (kvup[:, sl] + krope).astype(BF16)
    vc_ref[0] = kvup[:, C_HEADS * C_HEAD_PAD:].astype(BF16)

    zg = proj(_GATES, _W_IN_P)
    gates_ref[0] = (1.0 / (1.0 + jnp.exp(-zg))).astype(BF16)


def _inproj(stream, modsel, gpre, w_in_p, tabs, gqk, bd, gcq, gckv, wuq, wukv, n_ctx):
    B, D = stream[-1].shape[0], stream[-1].shape[2]
    N = tabs[0].shape[0]
    tr = ROW_TILE
    nct = n_ctx // tr
    row =lambda w: pl.BlockSpec((1, tr, w), lambda b, t: (b, t, 0))
    tab = pl.BlockSpec((tr, LANES), lambda b, t: (t, 0))
    outs = [("qa", 512, BF16), ("ka", 128, BF16), ("va", 128, BF16),
            ("qb", 256, F32), ("kb", 256, F32), ("vb", 512, BF16), ("misc", 128, BF16), ("gb", 512, F32),
            ("qc", C_HEADS * C_HEAD_PAD, BF16), ("kc", C_HEADS * C_HEAD_PAD, BF16),
            ("vc", C_HEADS * C_VDIM, BF16), ("gates", N_BRANCH * D, BF16)]
    return pl.pallas_call(
        functools.partial(_inproj_kernel, n_stream=len(stream), nct=nct),
        grid=(B, N // tr),
        in_specs=_stream_specs(tr, D, nct, len(stream) == 2) + [
            pl.BlockSpec((1, 1, N_MOD, D), lambda b, t: (b, jnp.where(t < nct, 0, 1), 0, 0)),
            _const_spec(gpre.shape), _const_spec(w_in_p.shape),
            tab, tab, tab, tab,
            _const_spec(gqk.shape), _const_spec(bd.shape), _const_spec(gcq.shape),
            _const_spec(gckv.shape), _const_spec(wuq.shape), _const_spec(wukv.shape)],
        out_specs=[row(w) for _, w, _ in outs],
        out_shape=[jax.ShapeDtypeStruct((B, N, w), dt) for _, w, dt in outs],
        compiler_params=_cparams(2),
        name="inproj",
    )(*stream, modsel, gpre, w_in_p, *tabs, gqk, bd, gcq, gckv, wuq, wukv)


def _attn_kernel(q_ref, k_ref, v_ref, o_ref, *, n_heads, q_w, kv_group, v_w, n_ctx, t0, nct):
    def body(n_keys):
        for g in range(n_heads // kv_group):
            k = k_ref[0, 0:n_keys, g * q_w:(g + 1) * q_w]
            v = v_ref[0, 0:n_keys, g * v_w:(g + 1) * v_w]
            for hh in range(g * kv_group, (g + 1) * kv_group):
                q = q_ref[0, :, hh * q_w:(hh + 1) * q_w]
                s = lax.dot_general(q, k, (((1,), (1,)), ((), ())), preferred_element_type=F32)
                p = jnp.exp2(s - jnp.max(s, axis=-1, keepdims=True))
                l = jnp.sum(p, axis=-1, keepdims=True)
                o = jnp.dot(p.astype(BF16), v, preferred_element_type=F32) / l
                o_ref[0, :, hh * v_w:(hh + 1) * v_w] = o.astype(o_ref.dtype)

    if t0 < nct:
        t = pl.program_id(1) + t0
        pl.when(t < nct)(lambda: body(n_ctx))
        pl.when(t >= nct)(lambda: body(k_ref.shape[1]))
    else:
        body(k_ref.shape[1])


def _attention(q, k, v, *, n_heads, q_w, kv_group, v_w, n_ctx, need_ctx, name):
    B, N, _ = q.shape
    tq = ROW_TILE
    nct = n_ctx // tq
    t0 = 0 if need_ctx else nct
    kern = functools.partial(_attn_kernel, n_heads=n_heads, q_w=q_w, kv_group=kv_group, v_w=v_w,
                             n_ctx=n_ctx, t0=t0, nct=nct)
    return pl.pallas_call(
        kern,
        grid=(B, N // tq - t0),
        in_specs=[pl.BlockSpec((1, tq, q.shape[2]), lambda b, t: (b, t + t0, 0)),
                  pl.BlockSpec((1, N, k.shape[2]), lambda b, t: (b, 0, 0)),
                  pl.BlockSpec((1, N, v.shape[2]), lambda b, t: (b, 0, 0))],
        out_specs=pl.BlockSpec((1, tq, n_heads * v_w), lambda b, t: (b, t + t0, 0)),
        out_shape=jax.ShapeDtypeStruct((B, N, n_heads * v_w), BF16),
        compiler_params=_cparams(2),
        name=name,
    )(q, k, v)


def _gla_kernel(q_ref, k_ref, v_ref, misc_ref, gate_ref, wal_ref, bal_ref, gout_ref, o_ref,
                la_scr, acc_scr, s_scr, *, n_ctx):
    n_rows = q_ref.shape[1]
    ck, blk = B_CHUNK, GLA_BLOCK
    cpb = blk // ck
    nb, ncb = n_rows // blk, n_ctx // blk
    kw = B_HEADS * B_KEY_DIM

    def decay_rows(i, carry):
        r0 = pl.multiple_of(i * blk, blk)
        z = jnp.dot(misc_ref[0, pl.ds(r0, blk), :], wal_ref[...], preferred_element_type=F32) + bal_ref[...]
        la_scr[pl.ds(r0, blk), :] = (jnp.minimum(z, 0.0) - jnp.log1p(jnp.exp(-jnp.abs(z)))) * (1.0 / B_GATE_NORM)
        return carry
    lax.fori_loop(0, nb, decay_rows, 0)

    ri = lax.broadcasted_iota(jnp.int32, (blk, blk), 0)
    ci = lax.broadcasted_iota(jnp.int32, (blk, blk), 1)
    same_chunk = (ri // ck) == (ci // ck)

    def block_step(ib, backward):
        r0 = pl.multiple_of(ib * blk, blk)
        rows = pl.ds(r0, blk)
        keep = same_chunk & ((ci >= ri) if backward else (ci <= ri))
        la = la_scr[rows, kw:2 * kw] if backward else la_scr[rows, 0:kw]
        tri = jnp.where(keep, 1.0, 0.0).astype(BF16)
        la_hi = la.astype(BF16)
        rem = la - la_hi.astype(F32)
        la_mid = rem.astype(BF16)
        la_lo = (rem - la_mid.astype(F32)).astype(BF16)
        b = (jnp.dot(tri, la_hi, preferred_element_type=F32)
             + jnp.dot(tri, la_mid, preferred_element_type=F32)
             + jnp.dot(tri, la_lo, preferred_element_type=F32))
        last = 0 if backward else ck - 1
        bl = jnp.concatenate([jnp.broadcast_to(b[j * ck + last:j * ck + last + 1, :], (ck, kw))
                              for j in range(cpb)], axis=0)
        qd = q_ref[0, rows, :] * jnp.exp(b)
        k = k_ref[0, rows, :]
        kd = k * jnp.exp(-b)
        ke_t = (k * jnp.exp(bl - b)).T
        g_t = jnp.exp(bl.T)
        v = v_ref[0, rows, :]
        order = range(cpb - 1, -1, -1) if backward else range(cpb)
        for hh in range(B_HEADS):
            ks = slice(hh * B_KEY_DIM, (hh + 1) * B_KEY_DIM)
            vs = slice(hh * B_VAL_DIM, (hh + 1) * B_VAL_DIM)
            qh, vh = qd[:, ks], v[:, vs]
            attn = lax.dot_general(qh.astype(BF16), kd[:, ks].astype(BF16), (((1,), (1,)), ((), ())),
                                   preferred_element_type=F32)
            o = jnp.dot(jnp.where(keep, attn, 0.0).astype(BF16), vh, preferred_element_type=F32)
            lhs_u = jnp.where(same_chunk, jnp.concatenate([ke_t[ks, :]] * cpb, axis=0), 0.0)
            u_all = jnp.dot(lhs_u.astype(BF16), vh, preferred_element_type=F32)
            s = s_scr[hh]
            s_enter = [None] * cpb
            for j in order:
                s_enter[j] = s.astype(BF16)
                s = g_t[ks, j * ck:j * ck + 1] * s + u_all[j * ck:(j + 1) * ck, :]
            s_scr[hh] = s
            qd4 = jnp.where(same_chunk, jnp.concatenate([qh] * cpb, axis=1), 0.0)
            o = o + jnp.dot(qd4.astype(BF16), jnp.concatenate(s_enter, axis=0), preferred_element_type=F32)
            if not backward:
                acc_scr[rows, vs] = o
            else:
                tot = _rms(acc_scr[rows, vs] + o, gout_ref[...])
                gt = gate_ref[0, rows, vs]
                o_ref[0, rows, vs] = (tot * (gt / (1.0 + jnp.exp(-gt)))).astype(o_ref.dtype)

    s_scr[...] = jnp.zeros_like(s_scr)
    def fwd(i, carry):
        block_step(i, False)
        return carry
    lax.fori_loop(0, nb, fwd, 0)

    s_scr[...] = jnp.zeros_like(s_scr)
    def bwd(i, carry):
        block_step(jnp.where(i < ncb, ncb - 1 - i, nb - 1 - (i - ncb)), True)
        return carry
    lax.fori_loop(0, nb, bwd, 0)


def _gla(qb, kb, vb, misc, gb, wal, bal, gout, n_ctx):
    B, N, _ = qb.shape
    vw = B_HEADS * B_VAL_DIM
    blk = lambda w: pl.BlockSpec((1, N, w), lambda b: (b, 0, 0))
    return pl.pallas_call(
        functools.partial(_gla_kernel, n_ctx=n_ctx),
        grid=(B,),
        in_specs=[blk(qb.shape[2]), blk(kb.shape[2]), blk(vw), blk(LANES), blk(vw),
                  _const_spec(wal.shape), _const_spec(bal.shape), _const_spec(gout.shape)],
        out_specs=blk(vw),
        out_shape=jax.ShapeDtypeStruct((B, N, vw), BF16),
        scratch_shapes=[pltpu.VMEM((N, 2 * B_HEADS * B_KEY_DIM), F32),
                        pltpu.VMEM((N, vw), F32),
                        pltpu.VMEM((B_HEADS, B_KEY_DIM, B_VAL_DIM), F32)],
        compiler_params=_cparams(1),
        name="gla",
    )(qb, kb, vb, misc, gb, wal, bal, gout)


def _post_kernel(*refs, n_stream, nct, t0):
    x_refs = refs[:n_stream]
    (mod_ref, oa_ref, ob_ref, oc_ref, gates_ref, wbr_ref, wout_ref, gpost_ref, gpre2_ref,
     w1_ref, w2_ref, gpost2_ref, o_ref) = refs[n_stream:]
    d = o_ref.shape[2]
    m = mod_ref[0, 0]
    acc = None
    for z, b_ref in enumerate((oa_ref, ob_ref, oc_ref)):
        proj = jnp.dot(b_ref[0], wbr_ref[z], preferred_element_type=F32)
        term = gates_ref[0, :, z * d:(z + 1) * d].astype(F32) * proj
        acc = term if acc is None else acc + term
    y = jnp.dot(acc.astype(BF16), wout_ref[...], preferred_element_type=F32)
    x1 = _stream_tile(x_refs, nct, t0) + m[2:3] * _rms(y, gpost_ref[...])
    h2 = (_rms(x1, gpre2_ref[...]) * (1.0 + m[4:5]) + m[3:4]).astype(BF16)
    u = jnp.dot(h2, w1_ref[...], preferred_element_type=F32)
    a = jnp.square(jnp.maximum(u, 0.0)).astype(BF16)
    f = jnp.dot(a, w2_ref[...], preferred_element_type=F32)
    o_ref[0] = x1 + m[5:6] * _rms(f, gpost2_ref[...])


def _post(stream, modsel, oa, ob, oc, gates, wbr, wout, gpost, gpre2, w1, w2, gpost2, n_ctx, need_ctx):
    B, N, _ = oa.shape
    D = stream[-1].shape[2]
    tr = ROW_TILE
    nct = n_ctx // tr
    t0 = 0 if need_ctx else nct
    row = lambda w: pl.BlockSpec((1, tr, w), lambda b, t: (b, t + t0, 0))
    mod = pl.BlockSpec((1, 1, N_MOD, D), lambda b, t: (b, jnp.where(t + t0 < nct, 0, 1), 0, 0))
    consts = (wbr, wout, gpost, gpre2, w1, w2, gpost2)
    return pl.pallas_call(
        functools.partial(_post_kernel, n_stream=len(stream), nct=nct, t0=t0),
        grid=(B, N // tr - t0),
        in_specs=_stream_specs(tr, D, nct, len(stream) == 2, t0) + [
            mod, row(BRANCH_W), row(BRANCH_W), row(BRANCH_W), row(N_BRANCH * D)] + [
            _const_spec(w.shape) for w in consts],
        out_specs=pl.BlockSpec((1, tr, D), lambda b, t: (b, t, 0)),
        out_shape=jax.ShapeDtypeStruct((B, N - t0 * tr, D), F32),
        compiler_params=_cparams(2),
        name="post",
    )(*stream, modsel, oa, ob, oc, gates, *consts)


def _pairs_split(w, n_heads):
    lead, hd = w.shape[:-1], w.shape[-1] // n_heads
    return w.reshape(lead + (n_heads, hd // 2, 2)).swapaxes(-1, -2).reshape(lead + (n_heads * hd,))


def _w_in_layout(w_in):
    L, D, _ = w_in.shape
    z32 = jnp.zeros((L, D, 32), w_in.dtype)
    misc = jnp.concatenate([w_in[..., 1792:1824], z32, _pairs_split(w_in[..., 2976:3008], 1), z32], axis=-1)
    parts = [_pairs_split(w_in[..., 0:512], A_HEADS), _pairs_split(w_in[..., 512:640], A_KV_HEADS),
             w_in[..., 640:1792], misc, w_in[..., 1824:2976], w_in[..., 3008:]]
    out = jnp.concatenate(parts, axis=-1).astype(BF16)
    assert out.shape[-1] == _W_IN_P
    return out


def _w_uq_layout(w_uq):
    L, R, _ = w_uq.shape
    w = w_uq.reshape(L, R, C_HEADS, C_NOPE + C_ROPE)
    rope = _pairs_split(w[..., C_NOPE:], 1)
    pad = jnp.zeros((L, R, C_HEADS, C_HEAD_PAD - C_NOPE - C_ROPE), w.dtype)
    return jnp.concatenate([w[..., :C_NOPE], rope, pad], axis=-1).reshape(L, R, C_HEADS * C_HEAD_PAD).astype(BF16)


def _w_ukv_layout(w_ukv):
    L, R, _ = w_ukv.shape
    w = w_ukv.reshape(L, R, C_HEADS, C_NOPE + C_VDIM)
    pad = jnp.zeros((L, R, C_HEADS, C_HEAD_PAD - C_NOPE), w.dtype)
    k = jnp.concatenate([w[..., :C_NOPE], pad], axis=-1).reshape(L, R, C_HEADS * C_HEAD_PAD)
    v = w[..., C_NOPE:].reshape(L, R, C_HEADS * C_VDIM)
    return jnp.concatenate([k, v], axis=-1).astype(BF16)


def _w_alpha_layout(w_alpha):
    L, _, r, kw = w_alpha.shape
    z = jnp.zeros((L, r, kw), w_alpha.dtype)
    top = jnp.concatenate([w_alpha[:, 0], z], axis=-1)
    bot = jnp.concatenate([z, w_alpha[:, 1]], axis=-1)
    return jnp.concatenate([top, bot, jnp.zeros((L, LANES - 2 * r, 2 * kw), w_alpha.dtype)], axis=1).astype(BF16)


def _rope_angles(n_lat, rot_dim):
    rows = n_lat // GRID_W
    row = jnp.repeat(jnp.arange(rows, dtype=F32), GRID_W)
    col = jnp.tile(jnp.arange(GRID_W, dtype=F32), rows)
    n_freq = rot_dim // 4
    inv_freq = ROPE_THETA ** (-jnp.arange(n_freq, dtype=F32) / n_freq)
    ang = jnp.concatenate([row[:, None] * inv_freq, col[:, None] * inv_freq], axis=-1)
    return jnp.cos(ang), jnp.sin(ang)


def _rope_tables(n_ctx, n_lat):
    cos, sin = _rope_angles(n_lat, A_HEAD_DIM)
    cos_a = jnp.concatenate([cos] * 4, axis=-1)
    sin_a = jnp.concatenate([-sin, sin, -sin, sin], axis=-1)
    cos, sin = _rope_angles(n_lat, C_ROPE)
    lo, hi = _MISC_ROPE_LO, LANES - _MISC_ROPE_LO - C_ROPE
    cos_c = jnp.concatenate([jnp.ones((n_lat, lo), F32), cos, cos, jnp.ones((n_lat, hi), F32)], axis=-1)
    sin_c = jnp.concatenate([jnp.zeros((n_lat, lo), F32), -sin, sin, jnp.zeros((n_lat, hi), F32)], axis=-1)
    ident = lambda t, v: jnp.concatenate([jnp.full((n_ctx, LANES), v, F32), t], axis=0)
    return ident(cos_a, 1.0), ident(sin_a, 0.0), ident(cos_c, 1.0), ident(sin_c, 0.0)


def kernel(x, c, ctx, c_ctx, w_ada, b_ada, g_pre_attn, g_post_attn, g_pre_mlp, g_post_mlp, w_in, a_g_q, a_g_k,
           b_w_alpha, b_b_alpha, b_g_out, c_g_q, c_g_kv, c_w_uq, c_w_ukv, w_branch, w_out, w_mlp_in, w_mlp_out):
    B, n_lat, D = x.shape
    n_ctx = ctx.shape[1]
    L = w_ada.shape[0]
    assert n_ctx % ROW_TILE == 0 and n_lat % ROW_TILE == 0
    assert n_ctx % GLA_BLOCK == 0 and n_lat % GLA_BLOCK == 0 and GLA_BLOCK % B_CHUNK == 0

    n_vec = -(-(B + 1) // 8) * 8
    cvec = jnp.concatenate([c, c_ctx[None], jnp.zeros((n_vec - B - 1, D), F32)], axis=0)
    mods = _ada(cvec, w_ada, b_ada)
    mod_lat = mods[:, :B].reshape(L, B, 1, N_MOD, D)
    mod_ctx = jnp.broadcast_to(mods[:, B].reshape(L, 1, 1, N_MOD, D), (L, B, 1, N_MOD, D))
    modsel = jnp.concatenate([mod_ctx, mod_lat], axis=2)

    tabs = _rope_tables(n_ctx, n_lat)
    w_in_p = _w_in_layout(w_in)
    gqk = jnp.concatenate([jnp.tile(_pairs_split(a_g_q, 1) * (A_HEAD_DIM ** -0.5 * LOG2E), (1, A_HEADS)),
                           jnp.tile(_pairs_split(a_g_k, 1), (1, A_KV_HEADS))], axis=-1)
    heads_per_bd = 256 // A_HEAD_DIM
    bd = jnp.kron(jnp.eye(heads_per_bd, dtype=F32), jnp.ones((A_HEAD_DIM, A_HEAD_DIM), F32)).astype(BF16)
    wuq, wukv = _w_uq_layout(c_w_uq), _w_ukv_layout(c_w_ukv)
    wal = _w_alpha_layout(b_w_alpha)
    bal = b_b_alpha.reshape(L, 1, 2 * B_HEADS * B_KEY_DIM)
    wbr, wout = w_branch.astype(BF16), w_out.astype(BF16)
    w1, w2 = w_mlp_in.astype(BF16), w_mlp_out.astype(BF16)
    vec = lambda g, l: g[l][None, :]

    stream = (ctx, x)
    for l in range(L):
        need_ctx = l < L - 1
        qa, ka, va, qb, kb, vb, misc, gb, qc, kc, vc, gates = _inproj(
            stream, modsel[l], vec(g_pre_attn, l), w_in_p[l], tabs, vec(gqk, l), bd,
            vec(c_g_q, l), vec(c_g_kv, l), wuq[l], wukv[l], n_ctx)
        oa = _attention(qa, ka, va, n_heads=A_HEADS, q_w=A_HEAD_DIM, kv_group=A_GROUP, v_w=A_HEAD_DIM,
                        n_ctx=n_ctx, need_ctx=need_ctx, name="attn_a")
        oc = _attention(qc, kc, vc, n_heads=C_HEADS, q_w=C_HEAD_PAD, kv_group=1, v_w=C_VDIM,
                        n_ctx=n_ctx, need_ctx=need_ctx, name="attn_c")
        ob = _gla(qb, kb, vb, misc, gb, wal[l], bal[l], vec(b_g_out, l), n_ctx)
        stream = (_post(stream, modsel[l], oa, ob, oc, gates, wbr[l], wout[l], vec(g_post_attn, l),
                        vec(g_pre_mlp, l), w1[l], w2[l], vec(g_post_mlp, l), n_ctx, need_ctx),)
    return stream[0]
```

```python
import functools

import jax
import jax.numpy as jnp
from jax import lax
from jax.experimental import pallas as pl
from jax.experimental.pallas import tpu as pltpu

EPS = 1e-6
ROPE_THETA = 10000.0
GRID_W = 64
N_MOD = 6
N_BRANCH = 3
A_HEADS, A_KV_HEADS, A_HEAD_DIM = 8, 2, 64
A_GROUP = A_HEADS // A_KV_HEADS
B_HEADS, B_KEY_DIM, B_VAL_DIM, B_GATE_RANK, B_GATE_NORM, B_CHUNK = 4, 64, 128, 16, 16.0, 64
C_HEADS, C_NOPE, C_ROPE, C_VDIM, C_Q_RANK, C_KV_RANK = 8, 64, 32, 64, 384, 256
BRANCH_W = 512

LANES = 128
V7X_VMEM_BYTES = 64 * 1024 * 1024
VMEM_LIMIT = V7X_VMEM_BYTES - 8 * 1024 * 1024

ROW_TILE = 256
GLA_BLOCK = 256
ATTN_LATENT_TILE = 512
C_HEAD_PAD = LANES

_QA, _KA, _VA = 0, 512, 640
_QB, _KB, _VB, _MISC, _GB = 768, 1024, 1280, 1792, 1920
_CQ, _CKV, _GATES, _W_IN_P = 2432, 2816, 3072, 6144
_MISC_ROPE_LO = 64

F32 = jnp.float32
BF16 = jnp.bfloat16
LOG2E = 1.4426950408889634


def _cparams(n_grid_axes):
    return pltpu.CompilerParams(dimension_semantics=("arbitrary",) * n_grid_axes,
                                vmem_limit_bytes=VMEM_LIMIT)


def _const_spec(shape):
    nd = len(shape)
    return pl.BlockSpec(shape, lambda *_: (0,) * nd, pipeline_mode=pl.Buffered(1))


def _stream_specs(arrs, tr, n_ctx, n_total, t0=0):
    nct, d = n_ctx // tr, arrs[0].shape[2]
    if len(arrs) == 2:
        return [pl.BlockSpec((1, tr, d), lambda b, t: (b, jnp.minimum(t + t0, nct - 1), 0)),
                pl.BlockSpec((1, tr, d), lambda b, t: (b, jnp.maximum(t + t0 - nct, 0), 0))]
    if arrs[0].shape[1] == n_total:
        return [pl.BlockSpec((1, tr, d), lambda b, t: (b, t + t0, 0))]
    assert arrs[0].shape[1] == n_total - n_ctx and t0 == nct
    return [pl.BlockSpec((1, tr, d), lambda b, t: (b, t, 0))]


def _stream_tile(refs, nct, t0=0):
    if len(refs) == 1:
        return refs[0][0]
    is_ctx = pl.program_id(1) + t0 < nct
    return jnp.where(is_ctx, refs[0][0], refs[1][0])


def _rms(x, g):
    return x * lax.rsqrt(jnp.mean(x * x, axis=-1, keepdims=True) + EPS) * g


def _swap_halves(x, half):
    lane = lax.broadcasted_iota(jnp.int32, x.shape, 1)
    first = (lane % (2 * half)) < half
    return jnp.where(first, pltpu.roll(x, LANES - half, 1), pltpu.roll(x, half, 1))


def _rope_slab(y, cos, sin, half):
    return y * cos + _swap_halves(y, half) * sin


def _ada_kernel(c_ref, w_ref, b_ref, o_ref):
    cv = c_ref[...]
    s = (cv / (1.0 + jnp.exp(-cv))).astype(BF16)
    o_ref[0] = jnp.dot(s, w_ref[0].astype(BF16), preferred_element_type=F32) + b_ref[0]


def _ada(cvec, w_ada, b_ada):
    L, D, W = w_ada.shape
    R = cvec.shape[0]
    tn = 1024
    return pl.pallas_call(
        _ada_kernel,
        grid=(L, W // tn),
        in_specs=[pl.BlockSpec((R, D), lambda l, j: (0, 0)),
                  pl.BlockSpec((1, D, tn), lambda l, j: (l, 0, j)),
                  pl.BlockSpec((1, 1, tn), lambda l, j: (l, 0, j))],
        out_specs=pl.BlockSpec((1, R, tn), lambda l, j: (l, 0, j)),
        out_shape=jax.ShapeDtypeStruct((L, R, W), F32),
        compiler_params=_cparams(2),
        name="ada",
    )(cvec, w_ada, b_ada.reshape(L, 1, W))


def _inproj_kernel(*refs, n_stream, nct):
    x_refs = refs[:n_stream]
    (mod_ref, gpre_ref, w_ref, cosa_ref, sina_ref, cosc_ref, sinc_ref,
     gqk_ref, bd_ref, gcq_ref, gckv_ref, wuq_ref, wukv_ref,
     qa_ref, ka_ref, va_ref, qb_ref, kb_ref, vb_ref, misc_ref, gb_ref,
     qc_ref, kc_ref, vc_ref, gates_ref) = refs[n_stream:]
    x = _stream_tile(x_refs, nct)
    m = mod_ref[0, 0]
    h = (_rms(x, gpre_ref[...]) * (1.0 + m[1:2]) + m[0:1]).astype(BF16)

    def proj(lo, hi):
        return jnp.dot(h, w_ref[:, lo:hi], preferred_element_type=F32)

    zg = proj(_GATES, _W_IN_P)
    gates_ref[0] = (1.0 / (1.0 + jnp.exp(-zg))).astype(BF16)

    za = proj(_QA, _QB)
    cosa, sina = cosa_ref[...], sina_ref[...]

    def headnorm_rope(z, gain, out_ref, out_lo):
        w = z.shape[1]
        sq = z * z
        sq_hi = sq.astype(BF16)
        sq_lo = (sq - sq_hi.astype(F32)).astype(BF16)
        bd = bd_ref[:w, :w]
        ss = (jnp.dot(sq_hi, bd, preferred_element_type=F32)
              + jnp.dot(sq_lo, bd, preferred_element_type=F32))
        y = z * lax.rsqrt(ss * (1.0 / A_HEAD_DIM) + EPS) * gain
        for s in range(w // LANES):
            ys = _rope_slab(y[:, s * LANES:(s + 1) * LANES], cosa, sina, A_HEAD_DIM // 2)
            out_ref[0, :, out_lo + s * LANES:out_lo + (s + 1) * LANES] = ys.astype(out_ref.dtype)

    headnorm_rope(za[:, 0:256], gqk_ref[:, 0:256], qa_ref, 0)
    headnorm_rope(za[:, 256:512], gqk_ref[:, 256:512], qa_ref, 256)
    headnorm_rope(za[:, 512:640], gqk_ref[:, 512:640], ka_ref, 0)
    va_ref[0] = za[:, 640:768].astype(BF16)

    zb = proj(_QB, _CQ)
    qb_ref[0] = zb[:, 0:256] * (B_KEY_DIM ** -0.5)
    kb_ref[0] = zb[:, 256:512]
    vb_ref[0] = zb[:, 512:1024].astype(BF16)
    cosc, sinc = cosc_ref[...], sinc_ref[...]
    misc = _rope_slab(zb[:, 1024:1152], cosc, sinc, C_ROPE // 2)
    misc_ref[0] = misc.astype(BF16)
    gb_ref[0] = zb[:, 1152:1664]

    zc = proj(_CQ, _GATES)
    cqn = _rms(zc[:, 0:C_Q_RANK], gcq_ref[...]).astype(BF16)
    qup = jnp.dot(cqn, wuq_ref[...], preferred_element_type=F32)
    scale_c = (C_NOPE + C_ROPE) ** -0.5 * LOG2E
    for hh in range(C_HEADS):
        sl = slice(hh * C_HEAD_PAD, (hh + 1) * C_HEAD_PAD)
        qc_ref[0, :, sl] = (_rope_slab(qup[:, sl], cosc, sinc, C_ROPE // 2) * scale_c).astype(BF16)
    ckvn = _rms(zc[:, C_Q_RANK:C_Q_RANK + C_KV_RANK], gckv_ref[...]).astype(BF16)
    kvup = jnp.dot(ckvn, wukv_ref[...], preferred_element_type=F32)
    lane = lax.broadcasted_iota(jnp.int32, misc.shape, 1)
    krope = jnp.where((lane >= _MISC_ROPE_LO) & (lane < _MISC_ROPE_LO + C_ROPE), misc, 0.0)
    for hh in range(C_HEADS):
        sl = slice(hh * C_HEAD_PAD, (hh + 1) * C_HEAD_PAD)
        kc_ref[0, :, sl] = (kvup[:, sl] + krope).astype(BF16)
    vc_ref[0] = kvup[:, C_HEADS * C_HEAD_PAD:].astype(BF16)


def _inproj(stream, modsel, gpre, w_in_p, tabs, gqk, bd, gcq, gckv, wuq, wukv, n_ctx):
    B, D = stream[-1].shape[0], stream[-1].shape[2]
    N = tabs[0].shape[0]
    tr = ROW_TILE
    nct = n_ctx // tr
    row =lambda w: pl.BlockSpec((1, tr, w), lambda b, t: (b, t, 0))
    tab = pl.BlockSpec((tr, LANES), lambda b, t: (t, 0))
    outs = [("qa", 512, BF16), ("ka", 128, BF16), ("va", 128, BF16),
            ("qb", 256, F32), ("kb", 256, F32), ("vb", 512, BF16), ("misc", 128, BF16), ("gb", 512, F32),
            ("qc", C_HEADS * C_HEAD_PAD, BF16), ("kc", C_HEADS * C_HEAD_PAD, BF16),
            ("vc", C_HEADS * C_VDIM, BF16), ("gates", N_BRANCH * D, BF16)]
    return pl.pallas_call(
        functools.partial(_inproj_kernel, n_stream=len(stream), nct=nct),
        grid=(B, N // tr),
        in_specs=_stream_specs(stream, tr, n_ctx, N) + [
            pl.BlockSpec((1, 1, N_MOD, D), lambda b, t: (b, jnp.where(t < nct, 0, 1), 0, 0)),
            _const_spec(gpre.shape), _const_spec(w_in_p.shape),
            tab, tab, tab, tab,
            _const_spec(gqk.shape), _const_spec(bd.shape), _const_spec(gcq.shape),
            _const_spec(gckv.shape), _const_spec(wuq.shape), _const_spec(wukv.shape)],
        out_specs=[row(w) for _, w, _ in outs],
        out_shape=[jax.ShapeDtypeStruct((B, N, w), dt) for _, w, dt in outs],
        compiler_params=_cparams(2),
        name="inproj",
    )(*stream, modsel, gpre, w_in_p, *tabs, gqk, bd, gcq, gckv, wuq, wukv)


def _attn_kernel(q_ref, k_ref, v_ref, o_ref, *, n_heads, q_w, kv_group, v_w):
    n_keys = k_ref.shape[1]

    def scores(hh):
        g = hh // kv_group
        return lax.dot_general(q_ref[0, :, hh * q_w:(hh + 1) * q_w], k_ref[0, :, g * q_w:(g + 1) * q_w],
                               (((1,), (1,)), ((), ())), preferred_element_type=F32)

    ones = jnp.ones((n_keys, LANES - v_w), BF16)
    v_aug = [jnp.concatenate([v_ref[0, :, g * v_w:(g + 1) * v_w], ones], axis=1)
             for g in range(n_heads // kv_group)]
    s_next = scores(0)
    for hh in range(n_heads):
        s = s_next
        if hh + 1 < n_heads:
            s_next = scores(hh + 1)
        p = jnp.exp2(s - jnp.max(s, axis=-1, keepdims=True))
        o = jnp.dot(p.astype(BF16), v_aug[hh // kv_group], preferred_element_type=F32)
        o_ref[0, :, hh * v_w:(hh + 1) * v_w] = (o[:, 0:v_w] / o[:, v_w:v_w + 1]).astype(o_ref.dtype)


def _attention(q, k, v, *, n_heads, q_w, kv_group, v_w, n_ctx, need_ctx, name):
    B, N, _ = q.shape
    n_lat, ow = N - n_ctx, n_heads * v_w
    kern = functools.partial(_attn_kernel, n_heads=n_heads, q_w=q_w, kv_group=kv_group, v_w=v_w)
    keys = lambda a, n: pl.BlockSpec((1, n, a.shape[2]), lambda b, t: (b, 0, 0))
    tq = ATTN_LATENT_TILE
    q_rows = pl.BlockSpec((pl.Element(1), pl.Element(tq), pl.Element(q.shape[2])),
                          lambda b, t: (b, pl.multiple_of(n_ctx + t * tq, ROW_TILE), 0))
    o_lat = pl.pallas_call(
        kern,
        grid=(B, n_lat // tq),
        in_specs=[q_rows, keys(k, N), keys(v, N)],
        out_specs=pl.BlockSpec((1, tq, ow), lambda b, t: (b, t, 0)),
        out_shape=jax.ShapeDtypeStruct((B, n_lat, ow), BF16),
        compiler_params=_cparams(2),
        name=name,
    )(q, k, v)
    if not need_ctx:
        return (o_lat,)
    tq = ROW_TILE
    rows = lambda w: pl.BlockSpec((1, tq, w), lambda b, t: (b, t, 0))
    o_ctx = pl.pallas_call(
        kern,
        grid=(B, n_ctx // tq),
        in_specs=[rows(q.shape[2]), keys(k, n_ctx), keys(v, n_ctx)],
        out_specs=rows(ow),
        out_shape=jax.ShapeDtypeStruct((B, n_ctx, ow), BF16),
        compiler_params=_cparams(2),
        name=name + "_ctx",
    )(q, k, v)
    return (o_ctx, o_lat)


def _gla_kernel(q_ref, k_ref, v_ref, misc_ref, gate_ref, wal_ref, bal_ref, gout_ref, o_ref,
                la_scr, acc_scr, s_scr, *, n_ctx):
    n_rows = q_ref.shape[1]
    ck, blk = B_CHUNK, GLA_BLOCK
    cpb = blk // ck
    nb, ncb = n_rows // blk, n_ctx // blk
    kw = B_HEADS * B_KEY_DIM

    def decay_rows(i, carry):
        r0 = pl.multiple_of(i * blk, blk)
        z = jnp.dot(misc_ref[0, pl.ds(r0, blk), :], wal_ref[...], preferred_element_type=F32) + bal_ref[...]
        la_scr[pl.ds(r0, blk), :] = (jnp.minimum(z, 0.0) - jnp.log1p(jnp.exp(-jnp.abs(z)))) * (1.0 / B_GATE_NORM)
        return carry
    lax.fori_loop(0, nb, decay_rows, 0)

    ri = lax.broadcasted_iota(jnp.int32, (blk, blk), 0)
    ci = lax.broadcasted_iota(jnp.int32, (blk, blk), 1)
    same_chunk = (ri // ck) == (ci // ck)

    def block_step(ib, backward):
        r0 = pl.multiple_of(ib * blk, blk)
        rows = pl.ds(r0, blk)
        keep = same_chunk & ((ci >= ri) if backward else (ci <= ri))
        la = la_scr[rows, kw:2 * kw] if backward else la_scr[rows, 0:kw]
        tri = jnp.where(keep, 1.0, 0.0).astype(BF16)
        la_hi = la.astype(BF16)
        rem = la - la_hi.astype(F32)
        la_mid = rem.astype(BF16)
        la_lo = (rem - la_mid.astype(F32)).astype(BF16)
        b = (jnp.dot(tri, la_hi, preferred_element_type=F32)
             + jnp.dot(tri, la_mid, preferred_element_type=F32)
             + jnp.dot(tri, la_lo, preferred_element_type=F32))
        last = 0 if backward else ck - 1
        bl = jnp.concatenate([jnp.broadcast_to(b[j * ck + last:j * ck + last + 1, :], (ck, kw))
                              for j in range(cpb)], axis=0)
        qd = q_ref[0, rows, :] * jnp.exp(b)
        k = k_ref[0, rows, :]
        kd = k * jnp.exp(-b)
        ke_t = (k * jnp.exp(bl - b)).T
        g_t = jnp.exp(bl.T)
        v = v_ref[0, rows, :]
        order = range(cpb - 1, -1, -1) if backward else range(cpb)
        heads = range(B_HEADS)
        ks = [slice(hh * B_KEY_DIM, (hh + 1) * B_KEY_DIM) for hh in heads]
        vs = [slice(hh * B_VAL_DIM, (hh + 1) * B_VAL_DIM) for hh in heads]
        attn = [lax.dot_general(qd[:, ks[hh]].astype(BF16), kd[:, ks[hh]].astype(BF16),
                                (((1,), (1,)), ((), ())), preferred_element_type=F32) for hh in heads]
        u_all = [jnp.dot(jnp.where(same_chunk, jnp.concatenate([ke_t[ks[hh], :]] * cpb, axis=0), 0.0).astype(BF16),
                         v[:, vs[hh]], preferred_element_type=F32) for hh in heads]
        o = [jnp.dot(jnp.where(keep, attn[hh], 0.0).astype(BF16), v[:, vs[hh]], preferred_element_type=F32)
             for hh in heads]
        s_stack = []
        for hh in heads:
            s = s_scr[hh]
            s_enter = [None] * cpb
            for j in order:
                s_enter[j] = s.astype(BF16)
                s = g_t[ks[hh], j * ck:j * ck + 1] * s + u_all[hh][j * ck:(j + 1) * ck, :]
            s_scr[hh] = s
            s_stack.append(jnp.concatenate(s_enter, axis=0))
        for hh in heads:
            qd4 = jnp.where(same_chunk, jnp.concatenate([qd[:, ks[hh]]] * cpb, axis=1), 0.0)
            tot = o[hh] + jnp.dot(qd4.astype(BF16), s_stack[hh], preferred_element_type=F32)
            if not backward:
                acc_scr[rows, vs[hh]] = tot
            else:
                tot = _rms(acc_scr[rows, vs[hh]] + tot, gout_ref[...])
                gt = gate_ref[0, rows, vs[hh]]
                o_ref[0, rows, vs[hh]] = (tot * (gt / (1.0 + jnp.exp(-gt)))).astype(o_ref.dtype)

    s_scr[...] = jnp.zeros_like(s_scr)
    def fwd(i, carry):
        block_step(i, False)
        return carry
    lax.fori_loop(0, nb, fwd, 0)

    s_scr[...] = jnp.zeros_like(s_scr)
    def bwd(i, carry):
        block_step(jnp.where(i < ncb, ncb - 1 - i, nb - 1 - (i - ncb)), True)
        return carry
    lax.fori_loop(0, nb, bwd, 0)


def _gla(qb, kb, vb, misc, gb, wal, bal, gout, n_ctx):
    B, N, _ = qb.shape
    vw = B_HEADS * B_VAL_DIM
    blk = lambda w: pl.BlockSpec((1, N, w), lambda b: (b, 0, 0))
    return pl.pallas_call(
        functools.partial(_gla_kernel, n_ctx=n_ctx),
        grid=(B,),
        in_specs=[blk(qb.shape[2]), blk(kb.shape[2]), blk(vw), blk(LANES), blk(vw),
                  _const_spec(wal.shape), _const_spec(bal.shape), _const_spec(gout.shape)],
        out_specs=blk(vw),
        out_shape=jax.ShapeDtypeStruct((B, N, vw), BF16),
        scratch_shapes=[pltpu.VMEM((N, 2 * B_HEADS * B_KEY_DIM), F32),
                        pltpu.VMEM((N, vw), F32),
                        pltpu.VMEM((B_HEADS, B_KEY_DIM, B_VAL_DIM), F32)],
        compiler_params=_cparams(1),
        name="gla",
    )(qb, kb, vb, misc, gb, wal, bal, gout)


def _post_kernel(*refs, n_stream, n_attn, nct, t0):
    x_refs, refs = refs[:n_stream], refs[n_stream:]
    mod_ref, refs = refs[0], refs[1:]
    oa_refs, ob_ref, oc_refs = refs[:n_attn], refs[n_attn], refs[n_attn + 1:2 * n_attn + 1]
    (gates_ref, wbr_ref, wout_ref, gpost_ref, gpre2_ref,
     w1_ref, w2_ref, gpost2_ref, o_ref) = refs[2 * n_attn + 1:]
    d = o_ref.shape[2]
    m = mod_ref[0, 0]
    branches = (_stream_tile(oa_refs, nct, t0), ob_ref[0], _stream_tile(oc_refs, nct, t0))
    acc = None
    for z, branch in enumerate(branches):
        proj = jnp.dot(branch, wbr_ref[z], preferred_element_type=F32)
        term = gates_ref[0, :, z * d:(z + 1) * d].astype(F32) * proj
        acc = term if acc is None else acc + term
    y = jnp.dot(acc.astype(BF16), wout_ref[...], preferred_element_type=F32)
    x1 = _stream_tile(x_refs, nct, t0) + m[2:3] * _rms(y, gpost_ref[...])
    h2 = (_rms(x1, gpre2_ref[...]) * (1.0 + m[4:5]) + m[3:4]).astype(BF16)
    u = jnp.dot(h2, w1_ref[...], preferred_element_type=F32)
    a = jnp.square(jnp.maximum(u, 0.0)).astype(BF16)
    f = jnp.dot(a, w2_ref[...], preferred_element_type=F32)
    o_ref[0] = x1 + m[5:6] * _rms(f, gpost2_ref[...])


def _post(stream, modsel, oa, ob, oc, gates, wbr, wout, gpost, gpre2, w1, w2, gpost2, n_ctx, need_ctx):
    B, N, _ = ob.shape
    D = stream[-1].shape[2]
    tr = ROW_TILE
    nct = n_ctx // tr
    t0 = 0 if need_ctx else nct
    specs = lambda arrs: _stream_specs(arrs, tr, n_ctx, N, t0)
    mod = pl.BlockSpec((1, 1, N_MOD, D), lambda b, t: (b, jnp.where(t + t0 < nct, 0, 1), 0, 0))
    consts = (wbr, wout, gpost, gpre2, w1, w2, gpost2)
    assert len(oa) == len(oc)
    return pl.pallas_call(
        functools.partial(_post_kernel, n_stream=len(stream), n_attn=len(oa), nct=nct, t0=t0),
        grid=(B, N // tr - t0),
        in_specs=specs(stream) + [mod] + specs(oa) + specs((ob,)) + specs(oc) + specs((gates,)) + [
            _const_spec(w.shape) for w in consts],
        out_specs=pl.BlockSpec((1, tr, D), lambda b, t: (b, t, 0)),
        out_shape=jax.ShapeDtypeStruct((B, N - t0 * tr, D), F32),
        compiler_params=_cparams(2),
        name="post",
    )(*stream, modsel, *oa, ob, *oc, gates, *consts)


def _pairs_split(w, n_heads):
    lead, hd = w.shape[:-1], w.shape[-1] // n_heads
    return w.reshape(lead + (n_heads, hd // 2, 2)).swapaxes(-1, -2).reshape(lead + (n_heads * hd,))


def _w_in_layout(w_in):
    L, D, _ = w_in.shape
    z32 = jnp.zeros((L, D, 32), w_in.dtype)
    misc = jnp.concatenate([w_in[..., 1792:1824], z32, _pairs_split(w_in[..., 2976:3008], 1), z32], axis=-1)
    parts = [_pairs_split(w_in[..., 0:512], A_HEADS), _pairs_split(w_in[..., 512:640], A_KV_HEADS),
             w_in[..., 640:1792], misc, w_in[..., 1824:2976], w_in[..., 3008:]]
    out = jnp.concatenate(parts, axis=-1).astype(BF16)
    assert out.shape[-1] == _W_IN_P
    return out


def _w_uq_layout(w_uq):
    L, R, _ = w_uq.shape
    w = w_uq.reshape(L, R, C_HEADS, C_NOPE + C_ROPE)
    rope = _pairs_split(w[..., C_NOPE:], 1)
    pad = jnp.zeros((L, R, C_HEADS, C_HEAD_PAD - C_NOPE - C_ROPE), w.dtype)
    return jnp.concatenate([w[..., :C_NOPE], rope, pad], axis=-1).reshape(L, R, C_HEADS * C_HEAD_PAD).astype(BF16)


def _w_ukv_layout(w_ukv):
    L, R, _ = w_ukv.shape
    w = w_ukv.reshape(L, R, C_HEADS, C_NOPE + C_VDIM)
    pad = jnp.zeros((L, R, C_HEADS, C_HEAD_PAD - C_NOPE), w.dtype)
    k = jnp.concatenate([w[..., :C_NOPE], pad], axis=-1).reshape(L, R, C_HEADS * C_HEAD_PAD)
    v = w[..., C_NOPE:].reshape(L, R, C_HEADS * C_VDIM)
    return jnp.concatenate([k, v], axis=-1).astype(BF16)


def _w_alpha_layout(w_alpha):
    L, _, r, kw = w_alpha.shape
    z = jnp.zeros((L, r, kw), w_alpha.dtype)
    top = jnp.concatenate([w_alpha[:, 0], z], axis=-1)
    bot = jnp.concatenate([z, w_alpha[:, 1]], axis=-1)
    return jnp.concatenate([top, bot, jnp.zeros((L, LANES - 2 * r, 2 * kw), w_alpha.dtype)], axis=1).astype(BF16)


def _rope_angles(n_lat, rot_dim):
    rows = n_lat // GRID_W
    row = jnp.repeat(jnp.arange(rows, dtype=F32), GRID_W)
    col = jnp.tile(jnp.arange(GRID_W, dtype=F32), rows)
    n_freq = rot_dim // 4
    inv_freq = ROPE_THETA ** (-jnp.arange(n_freq, dtype=F32) / n_freq)
    ang = jnp.concatenate([row[:, None] * inv_freq, col[:, None] * inv_freq], axis=-1)
    return jnp.cos(ang), jnp.sin(ang)


def _rope_tables(n_ctx, n_lat):
    cos, sin = _rope_angles(n_lat, A_HEAD_DIM)
    cos_a = jnp.concatenate([cos] * 4, axis=-1)
    sin_a = jnp.concatenate([-sin, sin, -sin, sin], axis=-1)
    cos, sin = _rope_angles(n_lat, C_ROPE)
    lo, hi = _MISC_ROPE_LO, LANES - _MISC_ROPE_LO - C_ROPE
    cos_c = jnp.concatenate([jnp.ones((n_lat, lo), F32), cos, cos, jnp.ones((n_lat, hi), F32)], axis=-1)
    sin_c = jnp.concatenate([jnp.zeros((n_lat, lo), F32), -sin, sin, jnp.zeros((n_lat, hi), F32)], axis=-1)
    ident = lambda t, v: jnp.concatenate([jnp.full((n_ctx, LANES), v, F32), t], axis=0)
    return ident(cos_a, 1.0), ident(sin_a, 0.0), ident(cos_c, 1.0), ident(sin_c, 0.0)


def kernel(x, c, ctx, c_ctx, w_ada, b_ada, g_pre_attn, g_post_attn, g_pre_mlp, g_post_mlp, w_in, a_g_q, a_g_k,
           b_w_alpha, b_b_alpha, b_g_out, c_g_q, c_g_kv, c_w_uq, c_w_ukv, w_branch, w_out, w_mlp_in, w_mlp_out):
    B, n_lat, D = x.shape
    n_ctx = ctx.shape[1]
    L = w_ada.shape[0]
    assert n_ctx % ROW_TILE == 0 and n_lat % ROW_TILE == 0
    assert n_ctx % GLA_BLOCK == 0 and n_lat % GLA_BLOCK == 0 and GLA_BLOCK % B_CHUNK == 0
    assert n_lat % ATTN_LATENT_TILE == 0

    n_vec = -(-(B + 1) // 8) * 8
    cvec = jnp.concatenate([c, c_ctx[None], jnp.zeros((n_vec - B - 1, D), F32)], axis=0)
    mods = _ada(cvec, w_ada, b_ada)
    mod_lat = mods[:, :B].reshape(L, B, 1, N_MOD, D)
    mod_ctx = jnp.broadcast_to(mods[:, B].reshape(L, 1, 1, N_MOD, D), (L, B, 1, N_MOD, D))
    modsel = jnp.concatenate([mod_ctx, mod_lat], axis=2)

    tabs = _rope_tables(n_ctx, n_lat)
    w_in_p = _w_in_layout(w_in)
    gqk = jnp.concatenate([jnp.tile(_pairs_split(a_g_q, 1) * (A_HEAD_DIM ** -0.5 * LOG2E), (1, A_HEADS)),
                           jnp.tile(_pairs_split(a_g_k, 1), (1, A_KV_HEADS))], axis=-1)
    heads_per_bd = 256 // A_HEAD_DIM
    bd = jnp.kron(jnp.eye(heads_per_bd, dtype=F32), jnp.ones((A_HEAD_DIM, A_HEAD_DIM), F32)).astype(BF16)
    wuq, wukv = _w_uq_layout(c_w_uq), _w_ukv_layout(c_w_ukv)
    wal = _w_alpha_layout(b_w_alpha)
    bal = b_b_alpha.reshape(L, 1, 2 * B_HEADS * B_KEY_DIM)
    wbr, wout = w_branch.astype(BF16), w_out.astype(BF16)
    w1, w2 = w_mlp_in.astype(BF16), w_mlp_out.astype(BF16)
    vec = lambda g, l: g[l][None, :]

    stream = (ctx, x)
    for l in range(L):
        need_ctx = l < L - 1
        qa, ka, va, qb, kb, vb, misc, gb, qc, kc, vc, gates = _inproj(
            stream, modsel[l], vec(g_pre_attn, l), w_in_p[l], tabs, vec(gqk, l), bd,
            vec(c_g_q, l), vec(c_g_kv, l), wuq[l], wukv[l], n_ctx)
        oa = _attention(qa, ka, va, n_heads=A_HEADS, q_w=A_HEAD_DIM, kv_group=A_GROUP, v_w=A_HEAD_DIM,
                        n_ctx=n_ctx, need_ctx=need_ctx, name="attn_a")
        oc = _attention(qc, kc, vc, n_heads=C_HEADS, q_w=C_HEAD_PAD, kv_group=1, v_w=C_VDIM,
                        n_ctx=n_ctx, need_ctx=need_ctx, name="attn_c")
        ob = _gla(qb, kb, vb, misc, gb, wal[l], bal[l], vec(b_g_out, l), n_ctx)
        stream = (_post(stream, modsel[l], oa, ob, oc, gates, wbr[l], wout[l], vec(g_post_attn, l),
                        vec(g_pre_mlp, l), w1[l], w2[l], vec(g_post_mlp, l), n_ctx, need_ctx),)
    return stream[0]
```

```python
import functools

import jax
import jax.numpy as jnp
from jax import lax
from jax.experimental import pallas as pl
from jax.experimental.pallas import tpu as pltpu

EPS = 1e-6
ROPE_THETA = 10000.0
GRID_W = 64
N_MOD = 6
N_BRANCH = 3
A_HEADS, A_KV_HEADS, A_HEAD_DIM = 8, 2, 64
A_GROUP = A_HEADS // A_KV_HEADS
B_HEADS, B_KEY_DIM, B_VAL_DIM, B_GATE_RANK, B_GATE_NORM, B_CHUNK = 4, 64, 128, 16, 16.0, 64
C_HEADS, C_NOPE, C_ROPE, C_VDIM, C_Q_RANK, C_KV_RANK = 8, 64, 32, 64, 384, 256
BRANCH_W = 512

LANES = 128
V7X_VMEM_BYTES = 64 * 1024 * 1024
VMEM_LIMIT = V7X_VMEM_BYTES - 8 * 1024 * 1024

ROW_TILE = 256
GLA_BLOCK = 256
ATTN_LATENT_TILE = 512
POST_SPLIT = 2
C_HEAD_PAD = LANES

_QA, _KA, _VA = 0, 512, 640
_QB, _KB, _VB, _MISC, _GB = 768, 1024, 1280, 1792, 1920
_CQ, _CKV, _GATES, _W_IN_P = 2432, 2816, 3072, 6144
_MISC_ROPE_LO = 64

F32 = jnp.float32
BF16 = jnp.bfloat16
LOG2E = 1.4426950408889634


def _cparams(n_grid_axes):
    return pltpu.CompilerParams(dimension_semantics=("arbitrary",) * n_grid_axes,
                                vmem_limit_bytes=VMEM_LIMIT)


def _const_spec(shape):
    nd = len(shape)
    return pl.BlockSpec(shape, lambda *_: (0,) * nd, pipeline_mode=pl.Buffered(1))


def _layer_spec(w, layer):
    nd = w.ndim
    return pl.BlockSpec((None,) + w.shape[1:], lambda *_: (layer,) + (0,) * (nd - 1),
                        pipeline_mode=pl.Buffered(1))


def _stream_specs(arrs, tr, n_ctx, n_total, t0=0):
    nct, d = n_ctx // tr, arrs[0].shape[2]
    if len(arrs) == 2:
        return [pl.BlockSpec((1, tr, d), lambda b, t: (b, jnp.minimum(t + t0, nct - 1), 0)),
                pl.BlockSpec((1, tr, d), lambda b, t: (b, jnp.maximum(t + t0 - nct, 0), 0))]
    if arrs[0].shape[1] == n_total:
        return [pl.BlockSpec((1, tr, d), lambda b, t: (b, t + t0, 0))]
    assert arrs[0].shape[1] == n_total - n_ctx and t0 == nct
    return [pl.BlockSpec((1, tr, d), lambda b, t: (b, t, 0))]


def _stream_tile(refs, nct, t0=0):
    if len(refs) == 1:
        return refs[0][0]
    is_ctx = pl.program_id(1) + t0 < nct
    return jnp.where(is_ctx, refs[0][0], refs[1][0])


def _rms(x, g):
    return x * lax.rsqrt(jnp.mean(x * x, axis=-1, keepdims=True) + EPS) * g


def _swap_halves(x, half):
    lane = lax.broadcasted_iota(jnp.int32, x.shape, 1)
    first = (lane % (2 * half)) < half
    return jnp.where(first, pltpu.roll(x, LANES - half, 1), pltpu.roll(x, half, 1))


def _rope_slab(y, cos, sin, half):
    return y * cos + _swap_halves(y, half) * sin


def _ada_kernel(c_ref, w_ref, b_ref, o_ref):
    cv = c_ref[...]
    s = (cv / (1.0 + jnp.exp(-cv))).astype(BF16)
    o_ref[0] = jnp.dot(s, w_ref[0].astype(BF16), preferred_element_type=F32) + b_ref[0]


def _ada(cvec, w_ada, b_ada):
    L, D, W = w_ada.shape
    R = cvec.shape[0]
    tn = 1024
    return pl.pallas_call(
        _ada_kernel,
        grid=(L, W // tn),
        in_specs=[pl.BlockSpec((R, D), lambda l, j: (0, 0)),
                  pl.BlockSpec((1, D, tn), lambda l, j: (l, 0, j)),
                  pl.BlockSpec((1, 1, tn), lambda l, j: (l, 0, j))],
        out_specs=pl.BlockSpec((1, R, tn), lambda l, j: (l, 0, j)),
        out_shape=jax.ShapeDtypeStruct((L, R, W), F32),
        compiler_params=_cparams(2),
        name="ada",
    )(cvec, w_ada, b_ada.reshape(L, 1, W))


def _inproj_kernel(*refs, n_stream, nct):
    x_refs = refs[:n_stream]
    (mod_ref, gpre_ref, w_ref, cosa_ref, sina_ref, cosc_ref, sinc_ref,
     gqk_ref, bd_ref, gcq_ref, gckv_ref, wuq_ref, wukv_ref,
     qa_ref, ka_ref, va_ref, qb_ref, kb_ref, vb_ref, misc_ref, gb_ref,
     qc_ref, kc_ref, vc_ref, gates_ref) = refs[n_stream:]
    x = _stream_tile(x_refs, nct)
    m = mod_ref[0, 0]
    h = (_rms(x, gpre_ref[...]) * (1.0 + m[1:2]) + m[0:1]).astype(BF16)

    def proj(lo, hi):
        return jnp.dot(h, w_ref[:, lo:hi], preferred_element_type=F32)

    zg = proj(_GATES, _W_IN_P)
    gates_ref[0] = (1.0 / (1.0 + jnp.exp(-zg))).astype(BF16)

    za = proj(_QA, _QB)
    cosa, sina = cosa_ref[...], sina_ref[...]

    def headnorm_rope(z, gain, out_ref, out_lo):
        w = z.shape[1]
        sq = z * z
        sq_hi = sq.astype(BF16)
        sq_lo = (sq - sq_hi.astype(F32)).astype(BF16)
        bd = bd_ref[:w, :w]
        ss = (jnp.dot(sq_hi, bd, preferred_element_type=F32)
              + jnp.dot(sq_lo, bd, preferred_element_type=F32))
        y = z * lax.rsqrt(ss * (1.0 / A_HEAD_DIM) + EPS) * gain
        for s in range(w // LANES):
            ys = _rope_slab(y[:, s * LANES:(s + 1) * LANES], cosa, sina, A_HEAD_DIM // 2)
            out_ref[0, :, out_lo + s * LANES:out_lo + (s + 1) * LANES] = ys.astype(out_ref.dtype)

    headnorm_rope(za[:, 0:256], gqk_ref[:, 0:256], qa_ref, 0)
    headnorm_rope(za[:, 256:512], gqk_ref[:, 256:512], qa_ref, 256)
    headnorm_rope(za[:, 512:640], gqk_ref[:, 512:640], ka_ref, 0)
    va_ref[0] = za[:, 640:768].astype(BF16)

    zb = proj(_QB, _CQ)
    qb_ref[0] = zb[:, 0:256] * (B_KEY_DIM ** -0.5)
    kb_ref[0] = zb[:, 256:512]
    vb_ref[0] = zb[:, 512:1024].astype(BF16)
    cosc, sinc = cosc_ref[...], sinc_ref[...]
    misc = _rope_slab(zb[:, 1024:1152], cosc, sinc, C_ROPE // 2)
    misc_ref[0] = misc.astype(BF16)
    gb_ref[0] = zb[:, 1152:1664]

    zc = proj(_CQ, _GATES)
    cqn = _rms(zc[:, 0:C_Q_RANK], gcq_ref[...]).astype(BF16)
    qup = jnp.dot(cqn, wuq_ref[...], preferred_element_type=F32)
    scale_c = (C_NOPE + C_ROPE) ** -0.5 * LOG2E
    for hh in range(C_HEADS):
        sl = slice(hh * C_HEAD_PAD, (hh + 1) * C_HEAD_PAD)
        qc_ref[0, :, sl] = (_rope_slab(qup[:, sl], cosc, sinc, C_ROPE // 2) * scale_c).astype(BF16)
    ckvn = _rms(zc[:, C_Q_RANK:C_Q_RANK + C_KV_RANK], gckv_ref[...]).astype(BF16)
    kvup = jnp.dot(ckvn, wukv_ref[...], preferred_element_type=F32)
    lane = lax.broadcasted_iota(jnp.int32, misc.shape, 1)
    krope = jnp.where((lane >= _MISC_ROPE_LO) & (lane < _MISC_ROPE_LO + C_ROPE), misc, 0.0)
    for hh in range(C_HEADS):
        sl = slice(hh * C_HEAD_PAD, (hh + 1) * C_HEAD_PAD)
        kc_ref[0, :, sl] = (kvup[:, sl] + krope).astype(BF16)
    vc_ref[0] = kvup[:, C_HEADS * C_HEAD_PAD:].astype(BF16)


def _inproj(stream, modsel, gpre, w_in_p, tabs, gqk, bd, gcq, gckv, wuq, wukv, n_ctx, layer):
    B, D = stream[-1].shape[0], stream[-1].shape[2]
    N = tabs[0].shape[0]
    tr = ROW_TILE
    nct = n_ctx // tr
    row =lambda w: pl.BlockSpec((1, tr, w), lambda b, t: (b, t, 0))
    tab = pl.BlockSpec((tr, LANES), lambda b, t: (t, 0))
    outs = [("qa", 512, BF16), ("ka", 128, BF16), ("va", 128, BF16),
            ("qb", 256, F32), ("kb", 256, F32), ("vb", 512, BF16), ("misc", 128, BF16), ("gb", 512, F32),
            ("qc", C_HEADS * C_HEAD_PAD, BF16), ("kc", C_HEADS * C_HEAD_PAD, BF16),
            ("vc", C_HEADS * C_VDIM, BF16), ("gates", N_BRANCH * D, BF16)]
    return pl.pallas_call(
        functools.partial(_inproj_kernel, n_stream=len(stream), nct=nct),
        grid=(B, N // tr),
        in_specs=_stream_specs(stream, tr, n_ctx, N) + [
            pl.BlockSpec((1, 1, N_MOD, D), lambda b, t: (b, jnp.where(t < nct, 0, 1), 0, 0)),
            _const_spec(gpre.shape), _layer_spec(w_in_p, layer),
            tab, tab, tab, tab,
            _const_spec(gqk.shape), _const_spec(bd.shape), _const_spec(gcq.shape),
            _const_spec(gckv.shape), _layer_spec(wuq, layer), _layer_spec(wukv, layer)],
        out_specs=[row(w) for _, w, _ in outs],
        out_shape=[jax.ShapeDtypeStruct((B, N, w), dt) for _, w, dt in outs],
        compiler_params=_cparams(2),
        name="inproj",
    )(*stream, modsel, gpre, w_in_p, *tabs, gqk, bd, gcq, gckv, wuq, wukv)


def _attn_kernel(q_ref, k_ref, v_ref, o_ref, *, n_heads, q_w, kv_group, v_w):
    n_keys = k_ref.shape[1]

    def scores(hh):
        g = hh // kv_group
        return lax.dot_general(q_ref[0, :, hh * q_w:(hh + 1) * q_w], k_ref[0, :, g * q_w:(g + 1) * q_w],
                               (((1,), (1,)), ((), ())), preferred_element_type=F32)

    ones = jnp.ones((n_keys, LANES - v_w), BF16)
    v_aug = [jnp.concatenate([v_ref[0, :, g * v_w:(g + 1) * v_w], ones], axis=1)
             for g in range(n_heads // kv_group)]
    s_next = scores(0)
    for hh in range(n_heads):
        s = s_next
        if hh + 1 < n_heads:
            s_next = scores(hh + 1)
        p = jnp.exp2(s - jnp.max(s, axis=-1, keepdims=True))
        o = jnp.dot(p.astype(BF16), v_aug[hh // kv_group], preferred_element_type=F32)
        o_ref[0, :, hh * v_w:(hh + 1) * v_w] = (o[:, 0:v_w] / o[:, v_w:v_w + 1]).astype(o_ref.dtype)


def _attention(q, k, v, *, n_heads, q_w, kv_group, v_w, n_ctx, need_ctx, name):
    B, N, _ = q.shape
    n_lat, ow = N - n_ctx, n_heads * v_w
    kern = functools.partial(_attn_kernel, n_heads=n_heads, q_w=q_w, kv_group=kv_group, v_w=v_w)
    keys = lambda a, n: pl.BlockSpec((1, n, a.shape[2]), lambda b, t: (b, 0, 0))
    tq = ATTN_LATENT_TILE
    q_rows = pl.BlockSpec((pl.Element(1), pl.Element(tq), pl.Element(q.shape[2])),
                          lambda b, t: (b, pl.multiple_of(n_ctx + t * tq, ROW_TILE), 0))
    o_lat = pl.pallas_call(
        kern,
        grid=(B, n_lat // tq),
        in_specs=[q_rows, keys(k, N), keys(v, N)],
        out_specs=pl.BlockSpec((1, tq, ow), lambda b, t: (b, t, 0)),
        out_shape=jax.ShapeDtypeStruct((B, n_lat, ow), BF16),
        compiler_params=_cparams(2),
        name=name,
    )(q, k, v)
    if not need_ctx:
        return (o_lat,)
    tq = ROW_TILE
    rows = lambda w: pl.BlockSpec((1, tq, w), lambda b, t: (b, t, 0))
    o_ctx = pl.pallas_call(
        kern,
        grid=(B, n_ctx // tq),
        in_specs=[rows(q.shape[2]), keys(k, n_ctx), keys(v, n_ctx)],
        out_specs=rows(ow),
        out_shape=jax.ShapeDtypeStruct((B, n_ctx, ow), BF16),
        compiler_params=_cparams(2),
        name=name + "_ctx",
    )(q, k, v)
    return (o_ctx, o_lat)


def _gla_kernel(q_ref, k_ref, v_ref, misc_ref, gate_ref, wal_ref, bal_ref, gout_ref, o_ref,
                la_scr, acc_scr, s_scr, *, n_ctx):
    n_rows = q_ref.shape[1]
    ck, blk = B_CHUNK, GLA_BLOCK
    cpb = blk // ck
    nb, ncb = n_rows // blk, n_ctx // blk
    kw = B_HEADS * B_KEY_DIM

    def decay_rows(i, carry):
        r0 = pl.multiple_of(i * blk, blk)
        z = jnp.dot(misc_ref[0, pl.ds(r0, blk), :], wal_ref[...], preferred_element_type=F32) + bal_ref[...]
        la_scr[pl.ds(r0, blk), :] = (jnp.minimum(z, 0.0) - jnp.log1p(jnp.exp(-jnp.abs(z)))) * (1.0 / B_GATE_NORM)
        return carry
    lax.fori_loop(0, nb, decay_rows, 0)

    ri = lax.broadcasted_iota(jnp.int32, (blk, blk), 0)
    ci = lax.broadcasted_iota(jnp.int32, (blk, blk), 1)
    same_chunk = (ri // ck) == (ci // ck)

    heads = range(B_HEADS)
    ks = [slice(hh * B_KEY_DIM, (hh + 1) * B_KEY_DIM) for hh in heads]
    vs = [slice(hh * B_VAL_DIM, (hh + 1) * B_VAL_DIM) for hh in heads]

    def front(ib, backward):
        rows = pl.ds(pl.multiple_of(ib * blk, blk), blk)
        keep = same_chunk & ((ci >= ri) if backward else (ci <= ri))
        la = la_scr[rows, kw:2 * kw] if backward else la_scr[rows, 0:kw]
        tri = jnp.where(keep, 1.0, 0.0).astype(BF16)
        la_hi = la.astype(BF16)
        rem = la - la_hi.astype(F32)
        la_mid = rem.astype(BF16)
        la_lo = (rem - la_mid.astype(F32)).astype(BF16)
        b = (jnp.dot(tri, la_hi, preferred_element_type=F32)
             + jnp.dot(tri, la_mid, preferred_element_type=F32)
             + jnp.dot(tri, la_lo, preferred_element_type=F32))
        last = 0 if backward else ck - 1
        bl = jnp.concatenate([jnp.broadcast_to(b[j * ck + last:j * ck + last + 1, :], (ck, kw))
                              for j in range(cpb)], axis=0)
        qd = q_ref[0, rows, :] * jnp.exp(b)
        k = k_ref[0, rows, :]
        kd = k * jnp.exp(-b)
        ke_t = (k * jnp.exp(bl - b)).T
        g_t = jnp.exp(bl.T)
        v = v_ref[0, rows, :]
        attn = [lax.dot_general(qd[:, ks[hh]].astype(BF16), kd[:, ks[hh]].astype(BF16),
                                (((1,), (1,)), ((), ())), preferred_element_type=F32) for hh in heads]
        u_all = [jnp.dot(jnp.where(same_chunk, jnp.concatenate([ke_t[ks[hh], :]] * cpb, axis=0), 0.0).astype(BF16),
                         v[:, vs[hh]], preferred_element_type=F32) for hh in heads]
        o = [jnp.dot(jnp.where(keep, attn[hh], 0.0).astype(BF16), v[:, vs[hh]], preferred_element_type=F32)
             for hh in heads]
        qd4 = [jnp.where(same_chunk, jnp.concatenate([qd[:, ks[hh]]] * cpb, axis=1), 0.0).astype(BF16)
               for hh in heads]
        return g_t, u_all, o, qd4

    def back(fr, ib, backward):
        g_t, u_all, o, qd4 = fr
        rows = pl.ds(pl.multiple_of(ib * blk, blk), blk)
        order = range(cpb - 1, -1, -1) if backward else range(cpb)
        d = int(backward)
        s_stack = []
        for hh in heads:
            s = s_scr[d, hh]
            s_enter = [None] * cpb
            for j in order:
                s_enter[j] = s.astype(BF16)
                s = g_t[ks[hh], j * ck:j * ck + 1] * s + u_all[hh][j * ck:(j + 1) * ck, :]
            s_scr[d, hh] = s
            s_stack.append(jnp.concatenate(s_enter, axis=0))
        for hh in heads:
            tot = o[hh] + jnp.dot(qd4[hh], s_stack[hh], preferred_element_type=F32)
            if not backward:
                acc_scr[rows, vs[hh]] = tot
            else:
                tot = _rms(acc_scr[rows, vs[hh]] + tot, gout_ref[...])
                gt = gate_ref[0, rows, vs[hh]]
                o_ref[0, rows, vs[hh]] = (tot * (gt / (1.0 + jnp.exp(-gt)))).astype(o_ref.dtype)

    s_scr[...] = jnp.zeros_like(s_scr)
    def fwd(i, carry):
        back(front(i, False), i, False)
        return carry
    lax.fori_loop(0, nb, fwd, 0)

    def bwd(i, carry):
        ib = jnp.where(i < ncb, ncb - 1 - i, nb - 1 - (i - ncb))
        back(front(ib, True), ib, True)
        return carry
    lax.fori_loop(0, nb, bwd, 0)


def _gla(qb, kb, vb, misc, gb, wal, bal, gout, n_ctx):
    B, N, _ = qb.shape
    vw = B_HEADS * B_VAL_DIM
    blk = lambda w: pl.BlockSpec((1, N, w), lambda b: (b, 0, 0))
    return pl.pallas_call(
        functools.partial(_gla_kernel, n_ctx=n_ctx),
        grid=(B,),
        in_specs=[blk(qb.shape[2]), blk(kb.shape[2]), blk(vw), blk(LANES), blk(vw),
                  _const_spec(wal.shape), _const_spec(bal.shape), _const_spec(gout.shape)],
        out_specs=blk(vw),
        out_shape=jax.ShapeDtypeStruct((B, N, vw), BF16),
        scratch_shapes=[pltpu.VMEM((N, 2 * B_HEADS * B_KEY_DIM), F32),
                        pltpu.VMEM((N, vw), F32),
                        pltpu.VMEM((2, B_HEADS, B_KEY_DIM, B_VAL_DIM), F32)],
        compiler_params=_cparams(1),
        name="gla",
    )(qb, kb, vb, misc, gb, wal, bal, gout)


def _post_kernel(*refs, n_stream, n_attn, nct, t0):
    x_refs, refs = refs[:n_stream], refs[n_stream:]
    mod_ref, refs = refs[0], refs[1:]
    oa_refs, ob_ref, oc_refs = refs[:n_attn], refs[n_attn], refs[n_attn + 1:2 * n_attn + 1]
    (gates_ref, wbr_ref, wout_ref, gpost_ref, gpre2_ref,
     w1_ref, w2_ref, gpost2_ref, o_ref) = refs[2 * n_attn + 1:]
    d = o_ref.shape[2]
    m = mod_ref[0, 0]
    tr = o_ref.shape[1]
    halves = [slice(i * tr // POST_SPLIT, (i + 1) * tr // POST_SPLIT) for i in range(POST_SPLIT)]
    branches = (_stream_tile(oa_refs, nct, t0), ob_ref[0], _stream_tile(oc_refs, nct, t0))
    x = _stream_tile(x_refs, nct, t0)
    acc = []
    for r in halves:
        tot = None
        for z, branch in enumerate(branches):
            proj = jnp.dot(branch[r], wbr_ref[z], preferred_element_type=F32)
            term = gates_ref[0, r, z * d:(z + 1) * d].astype(F32) * proj
            tot = term if tot is None else tot + term
        acc.append(tot.astype(BF16))
    y = [jnp.dot(a, wout_ref[...], preferred_element_type=F32) for a in acc]
    x1 = [x[r] + m[2:3] * _rms(yh, gpost_ref[...]) for r, yh in zip(halves, y)]
    h2 = [(_rms(xh, gpre2_ref[...]) * (1.0 + m[4:5]) + m[3:4]).astype(BF16) for xh in x1]
    u = [jnp.dot(h, w1_ref[...], preferred_element_type=F32) for h in h2]
    a = [jnp.square(jnp.maximum(uh, 0.0)).astype(BF16) for uh in u]
    f = [jnp.dot(ah, w2_ref[...], preferred_element_type=F32) for ah in a]
    for r, xh, fh in zip(halves, x1, f):
        o_ref[0, r, :] = xh + m[5:6] * _rms(fh, gpost2_ref[...])


def _post(stream, modsel, oa, ob, oc, gates, wbr, wout, gpost, gpre2, w1, w2, gpost2, n_ctx, need_ctx, layer):
    B, N, _ = ob.shape
    D = stream[-1].shape[2]
    tr = ROW_TILE
    nct = n_ctx // tr
    t0 = 0 if need_ctx else nct
    specs = lambda arrs: _stream_specs(arrs, tr, n_ctx, N, t0)
    mod = pl.BlockSpec((1, 1, N_MOD, D), lambda b, t: (b, jnp.where(t + t0 < nct, 0, 1), 0, 0))
    consts = (wbr, wout, gpost, gpre2, w1, w2, gpost2)
    const_specs = [_layer_spec(w, layer) if w.ndim > 2 else _const_spec(w.shape) for w in consts]
    assert len(oa) == len(oc)
    return pl.pallas_call(
        functools.partial(_post_kernel, n_stream=len(stream), n_attn=len(oa), nct=nct, t0=t0),
        grid=(B, N // tr - t0),
        in_specs=specs(stream) + [mod] + specs(oa) + specs((ob,)) + specs(oc) + specs((gates,)) + const_specs,
        out_specs=pl.BlockSpec((1, tr, D), lambda b, t: (b, t, 0)),
        out_shape=jax.ShapeDtypeStruct((B, N - t0 * tr, D), F32),
        compiler_params=_cparams(2),
        name="post",
    )(*stream, modsel, *oa, ob, *oc, gates, *consts)


def _pairs_split(w, n_heads):
    lead, hd = w.shape[:-1], w.shape[-1] // n_heads
    return w.reshape(lead + (n_heads, hd // 2, 2)).swapaxes(-1, -2).reshape(lead + (n_heads * hd,))


def _w_in_layout(w_in):
    L, D, _ = w_in.shape
    z32 = jnp.zeros((L, D, 32), w_in.dtype)
    misc = jnp.concatenate([w_in[..., 1792:1824], z32, _pairs_split(w_in[..., 2976:3008], 1), z32], axis=-1)
    parts = [_pairs_split(w_in[..., 0:512], A_HEADS), _pairs_split(w_in[..., 512:640], A_KV_HEADS),
             w_in[..., 640:1792], misc, w_in[..., 1824:2976], w_in[..., 3008:]]
    out = jnp.concatenate(parts, axis=-1).astype(BF16)
    assert out.shape[-1] == _W_IN_P
    return out


def _w_uq_layout(w_uq):
    L, R, _ = w_uq.shape
    w = w_uq.reshape(L, R, C_HEADS, C_NOPE + C_ROPE)
    rope = _pairs_split(w[..., C_NOPE:], 1)
    pad = jnp.zeros((L, R, C_HEADS, C_HEAD_PAD - C_NOPE - C_ROPE), w.dtype)
    return jnp.concatenate([w[..., :C_NOPE], rope, pad], axis=-1).reshape(L, R, C_HEADS * C_HEAD_PAD).astype(BF16)


def _w_ukv_layout(w_ukv):
    L, R, _ = w_ukv.shape
    w = w_ukv.reshape(L, R, C_HEADS, C_NOPE + C_VDIM)
    pad = jnp.zeros((L, R, C_HEADS, C_HEAD_PAD - C_NOPE), w.dtype)
    k = jnp.concatenate([w[..., :C_NOPE], pad], axis=-1).reshape(L, R, C_HEADS * C_HEAD_PAD)
    v = w[..., C_NOPE:].reshape(L, R, C_HEADS * C_VDIM)
    return jnp.concatenate([k, v], axis=-1).astype(BF16)


def _w_alpha_layout(w_alpha):
    L, _, r, kw = w_alpha.shape
    z = jnp.zeros((L, r, kw), w_alpha.dtype)
    top = jnp.concatenate([w_alpha[:, 0], z], axis=-1)
    bot = jnp.concatenate([z, w_alpha[:, 1]], axis=-1)
    return jnp.concatenate([top, bot, jnp.zeros((L, LANES - 2 * r, 2 * kw), w_alpha.dtype)], axis=1).astype(BF16)


def _rope_angles(n_lat, rot_dim):
    rows = n_lat // GRID_W
    row = jnp.repeat(jnp.arange(rows, dtype=F32), GRID_W)
    col = jnp.tile(jnp.arange(GRID_W, dtype=F32), rows)
    n_freq = rot_dim // 4
    inv_freq = ROPE_THETA ** (-jnp.arange(n_freq, dtype=F32) / n_freq)
    ang = jnp.concatenate([row[:, None] * inv_freq, col[:, None] * inv_freq], axis=-1)
    return jnp.cos(ang), jnp.sin(ang)


def _rope_tables(n_ctx, n_lat):
    cos, sin = _rope_angles(n_lat, A_HEAD_DIM)
    cos_a = jnp.concatenate([cos] * 4, axis=-1)
    sin_a = jnp.concatenate([-sin, sin, -sin, sin], axis=-1)
    cos, sin = _rope_angles(n_lat, C_ROPE)
    lo, hi = _MISC_ROPE_LO, LANES - _MISC_ROPE_LO - C_ROPE
    cos_c = jnp.concatenate([jnp.ones((n_lat, lo), F32), cos, cos, jnp.ones((n_lat, hi), F32)], axis=-1)
    sin_c = jnp.concatenate([jnp.zeros((n_lat, lo), F32), -sin, sin, jnp.zeros((n_lat, hi), F32)], axis=-1)
    ident = lambda t, v: jnp.concatenate([jnp.full((n_ctx, LANES), v, F32), t], axis=0)
    return ident(cos_a, 1.0), ident(sin_a, 0.0), ident(cos_c, 1.0), ident(sin_c, 0.0)


def kernel(x, c, ctx, c_ctx, w_ada, b_ada, g_pre_attn, g_post_attn, g_pre_mlp, g_post_mlp, w_in, a_g_q, a_g_k,
           b_w_alpha, b_b_alpha, b_g_out, c_g_q, c_g_kv, c_w_uq, c_w_ukv, w_branch, w_out, w_mlp_in, w_mlp_out):
    B, n_lat, D = x.shape
    n_ctx = ctx.shape[1]
    L = w_ada.shape[0]
    assert n_ctx % ROW_TILE == 0 and n_lat % ROW_TILE == 0
    assert n_ctx % GLA_BLOCK == 0 and n_lat % GLA_BLOCK == 0 and GLA_BLOCK % B_CHUNK == 0
    assert n_lat % ATTN_LATENT_TILE == 0

    n_vec = -(-(B + 1) // 8) * 8
    cvec = jnp.concatenate([c, c_ctx[None], jnp.zeros((n_vec - B - 1, D), F32)], axis=0)
    mods = _ada(cvec, w_ada, b_ada)
    mod_lat = mods[:, :B].reshape(L, B, 1, N_MOD, D)
    mod_ctx = jnp.broadcast_to(mods[:, B].reshape(L, 1, 1, N_MOD, D), (L, B, 1, N_MOD, D))
    modsel = jnp.concatenate([mod_ctx, mod_lat], axis=2)

    tabs = _rope_tables(n_ctx, n_lat)
    w_in_p = _w_in_layout(w_in)
    gqk = jnp.concatenate([jnp.tile(_pairs_split(a_g_q, 1) * (A_HEAD_DIM ** -0.5 * LOG2E), (1, A_HEADS)),
                           jnp.tile(_pairs_split(a_g_k, 1), (1, A_KV_HEADS))], axis=-1)
    heads_per_bd = 256 // A_HEAD_DIM
    bd = jnp.kron(jnp.eye(heads_per_bd, dtype=F32), jnp.ones((A_HEAD_DIM, A_HEAD_DIM), F32)).astype(BF16)
    wuq, wukv = _w_uq_layout(c_w_uq), _w_ukv_layout(c_w_ukv)
    wal = _w_alpha_layout(b_w_alpha)
    bal = b_b_alpha.reshape(L, 1, 2 * B_HEADS * B_KEY_DIM)
    wbr, wout = w_branch.astype(BF16), w_out.astype(BF16)
    w1, w2 = w_mlp_in.astype(BF16), w_mlp_out.astype(BF16)
    vec = lambda g, l: g[l][None, :]

    stream = (ctx, x)
    for l in range(L):
        need_ctx = l < L - 1
        qa, ka, va, qb, kb, vb, misc, gb, qc, kc, vc, gates = _inproj(
            stream, modsel[l], vec(g_pre_attn, l), w_in_p, tabs, vec(gqk, l), bd,
            vec(c_g_q, l), vec(c_g_kv, l), wuq, wukv, n_ctx, l)
        oa = _attention(qa, ka, va, n_heads=A_HEADS, q_w=A_HEAD_DIM, kv_group=A_GROUP, v_w=A_HEAD_DIM,
                        n_ctx=n_ctx, need_ctx=need_ctx, name="attn_a")
        oc = _attention(qc, kc, vc, n_heads=C_HEADS, q_w=C_HEAD_PAD, kv_group=1, v_w=C_VDIM,
                        n_ctx=n_ctx, need_ctx=need_ctx, name="attn_c")
        ob = _gla(qb, kb, vb, misc, gb, wal[l], bal[l], vec(b_g_out, l), n_ctx)
        stream = (_post(stream, modsel[l], oa, ob, oc, gates, wbr, wout, vec(g_post_attn, l),
                        vec(g_pre_mlp, l), w1, w2, vec(g_post_mlp, l), n_ctx, need_ctx, l),)
    return stream[0]
```

```python
import functools

import jax
import jax.numpy as jnp
from jax import lax
from jax.experimental import pallas as pl
from jax.experimental.pallas import tpu as pltpu

EPS = 1e-6
ROPE_THETA = 10000.0
GRID_W = 64
N_MOD = 6
N_BRANCH = 3
A_HEADS, A_KV_HEADS, A_HEAD_DIM = 8, 2, 64
A_GROUP = A_HEADS // A_KV_HEADS
B_HEADS, B_KEY_DIM, B_VAL_DIM, B_GATE_RANK, B_GATE_NORM, B_CHUNK = 4, 64, 128, 16, 16.0, 64
C_HEADS, C_NOPE, C_ROPE, C_VDIM, C_Q_RANK, C_KV_RANK = 8, 64, 32, 64, 384, 256
BRANCH_W = 512

LANES = 128
V7X_VMEM_BYTES = 64 * 1024 * 1024
VMEM_LIMIT = V7X_VMEM_BYTES - 8 * 1024 * 1024

ROW_TILE = 256
GLA_BLOCK = 256
ATTN_LATENT_TILE = 512
POST_SPLIT = 2
C_HEAD_PAD = LANES

_QA, _KA, _VA = 0, 512, 640
_QB, _KB, _VB, _MISC, _GB = 768, 1024, 1280, 1792, 1920
_CQ, _CKV, _GATES, _W_IN_P = 2432, 2816, 3072, 6144
_MISC_ROPE_LO = 64

F32 = jnp.float32
BF16 = jnp.bfloat16
LOG2E = 1.4426950408889634


def _cparams(n_grid_axes):
    return pltpu.CompilerParams(dimension_semantics=("arbitrary",) * n_grid_axes,
                                vmem_limit_bytes=VMEM_LIMIT)


def _const_spec(shape):
    nd = len(shape)
    return pl.BlockSpec(shape, lambda *_: (0,) * nd, pipeline_mode=pl.Buffered(1))


def _layer_spec(w, layer):
    nd = w.ndim
    return pl.BlockSpec((None,) + w.shape[1:], lambda *_: (layer,) + (0,) * (nd - 1),
                        pipeline_mode=pl.Buffered(1))


def _stream_specs(arrs, tr, n_ctx, n_total, t0=0):
    nct, d = n_ctx // tr, arrs[0].shape[2]
    if len(arrs) == 2:
        return [pl.BlockSpec((1, tr, d), lambda b, t: (b, jnp.minimum(t + t0, nct - 1), 0)),
                pl.BlockSpec((1, tr, d), lambda b, t: (b, jnp.maximum(t + t0 - nct, 0), 0))]
    if arrs[0].shape[1] == n_total:
        return [pl.BlockSpec((1, tr, d), lambda b, t: (b, t + t0, 0))]
    assert arrs[0].shape[1] == n_total - n_ctx and t0 == nct
    return [pl.BlockSpec((1, tr, d), lambda b, t: (b, t, 0))]


def _stream_tile(refs, nct, t0=0):
    if len(refs) == 1:
        return refs[0][0]
    is_ctx = pl.program_id(1) + t0 < nct
    return jnp.where(is_ctx, refs[0][0], refs[1][0])


def _rms(x, g):
    return x * lax.rsqrt(jnp.mean(x * x, axis=-1, keepdims=True) + EPS) * g


def _swap_halves(x, half):
    lane = lax.broadcasted_iota(jnp.int32, x.shape, 1)
    first = (lane % (2 * half)) < half
    return jnp.where(first, pltpu.roll(x, LANES - half, 1), pltpu.roll(x, half, 1))


def _rope_slab(y, cos, sin, half):
    return y * cos + _swap_halves(y, half) * sin


def _ada_kernel(c_ref, w_ref, b_ref, o_ref):
    cv = c_ref[...]
    s = (cv / (1.0 + jnp.exp(-cv))).astype(BF16)
    o_ref[0] = jnp.dot(s, w_ref[0].astype(BF16), preferred_element_type=F32) + b_ref[0]


def _ada(cvec, w_ada, b_ada):
    L, D, W = w_ada.shape
    R = cvec.shape[0]
    tn = 1024
    return pl.pallas_call(
        _ada_kernel,
        grid=(L, W // tn),
        in_specs=[pl.BlockSpec((R, D), lambda l, j: (0, 0)),
                  pl.BlockSpec((1, D, tn), lambda l, j: (l, 0, j)),
                  pl.BlockSpec((1, 1, tn), lambda l, j: (l, 0, j))],
        out_specs=pl.BlockSpec((1, R, tn), lambda l, j: (l, 0, j)),
        out_shape=jax.ShapeDtypeStruct((L, R, W), F32),
        compiler_params=_cparams(2),
        name="ada",
    )(cvec, w_ada, b_ada.reshape(L, 1, W))


def _inproj_kernel(*refs, n_stream, nct):
    x_refs = refs[:n_stream]
    (mod_ref, gpre_ref, w_ref, cosa_ref, sina_ref, cosc_ref, sinc_ref,
     gqk_ref, bd_ref, gcq_ref, gckv_ref, wuq_ref, wukv_ref,
     qa_ref, ka_ref, va_ref, qb_ref, kb_ref, vb_ref, misc_ref, gb_ref,
     qc_ref, kc_ref, vc_ref, gates_ref) = refs[n_stream:]
    x = _stream_tile(x_refs, nct)
    m = mod_ref[0, 0]
    h = (_rms(x, gpre_ref[...]) * (1.0 + m[1:2]) + m[0:1]).astype(BF16)

    def proj(lo, hi):
        return jnp.dot(h, w_ref[:, lo:hi], preferred_element_type=F32)

    zg = proj(_GATES, _W_IN_P)
    gates_ref[0] = (1.0 / (1.0 + jnp.exp(-zg))).astype(BF16)

    za = proj(_QA, _QB)
    cosa, sina = cosa_ref[...], sina_ref[...]

    def headnorm_rope(z, gain, out_ref, out_lo):
        w = z.shape[1]
        sq = z * z
        sq_hi = sq.astype(BF16)
        sq_lo = (sq - sq_hi.astype(F32)).astype(BF16)
        bd = bd_ref[:w, :w]
        ss = (jnp.dot(sq_hi, bd, preferred_element_type=F32)
              + jnp.dot(sq_lo, bd, preferred_element_type=F32))
        y = z * lax.rsqrt(ss * (1.0 / A_HEAD_DIM) + EPS) * gain
        for s in range(w // LANES):
            ys = _rope_slab(y[:, s * LANES:(s + 1) * LANES], cosa, sina, A_HEAD_DIM // 2)
            out_ref[0, :, out_lo + s * LANES:out_lo + (s + 1) * LANES] = ys.astype(out_ref.dtype)

    headnorm_rope(za[:, 0:256], gqk_ref[:, 0:256], qa_ref, 0)
    headnorm_rope(za[:, 256:512], gqk_ref[:, 256:512], qa_ref, 256)
    headnorm_rope(za[:, 512:640], gqk_ref[:, 512:640], ka_ref, 0)
    va_ref[0] = za[:, 640:768].astype(BF16)

    zb = proj(_QB, _CQ)
    qb_ref[0] = zb[:, 0:256] * (B_KEY_DIM ** -0.5)
    kb_ref[0] = zb[:, 256:512]
    vb_ref[0] = zb[:, 512:1024].astype(BF16)
    cosc, sinc = cosc_ref[...], sinc_ref[...]
    misc = _rope_slab(zb[:, 1024:1152], cosc, sinc, C_ROPE // 2)
    misc_ref[0] = misc.astype(BF16)
    gb_ref[0] = zb[:, 1152:1664]

    zc = proj(_CQ, _GATES)
    cqn = _rms(zc[:, 0:C_Q_RANK], gcq_ref[...]).astype(BF16)
    qup = jnp.dot(cqn, wuq_ref[...], preferred_element_type=F32)
    scale_c = (C_NOPE + C_ROPE) ** -0.5 * LOG2E
    for hh in range(C_HEADS):
        sl = slice(hh * C_HEAD_PAD, (hh + 1) * C_HEAD_PAD)
        qc_ref[0, :, sl] = (_rope_slab(qup[:, sl], cosc, sinc, C_ROPE // 2) * scale_c).astype(BF16)
    ckvn = _rms(zc[:, C_Q_RANK:C_Q_RANK + C_KV_RANK], gckv_ref[...]).astype(BF16)
    kvup = jnp.dot(ckvn, wukv_ref[...], preferred_element_type=F32)
    lane = lax.broadcasted_iota(jnp.int32, misc.shape, 1)
    krope = jnp.where((lane >= _MISC_ROPE_LO) & (lane < _MISC_ROPE_LO + C_ROPE), misc, 0.0)
    for hh in range(C_HEADS):
        sl = slice(hh * C_HEAD_PAD, (hh + 1) * C_HEAD_PAD)
        kc_ref[0, :, sl] = (kvup[:, sl] + krope).astype(BF16)
    vc_ref[0] = kvup[:, C_HEADS * C_HEAD_PAD:].astype(BF16)


def _inproj(stream, modsel, gpre, w_in_p, tabs, gqk, bd, gcq, gckv, wuq, wukv, n_ctx, layer):
    B, D = stream[-1].shape[0], stream[-1].shape[2]
    N = tabs[0].shape[0]
    tr = ROW_TILE
    nct = n_ctx // tr
    row =lambda w: pl.BlockSpec((1, tr, w), lambda b, t: (b, t, 0))
    tab = pl.BlockSpec((tr, LANES), lambda b, t: (t, 0))
    outs = [("qa", 512, BF16), ("ka", 128, BF16), ("va", 128, BF16),
            ("qb", 256, F32), ("kb", 256, F32), ("vb", 512, BF16), ("misc", 128, BF16), ("gb", 512, F32),
            ("qc", C_HEADS * C_HEAD_PAD, BF16), ("kc", C_HEADS * C_HEAD_PAD, BF16),
            ("vc", C_HEADS * C_VDIM, BF16), ("gates", N_BRANCH * D, BF16)]
    return pl.pallas_call(
        functools.partial(_inproj_kernel, n_stream=len(stream), nct=nct),
        grid=(B, N // tr),
        in_specs=_stream_specs(stream, tr, n_ctx, N) + [
            pl.BlockSpec((1, 1, N_MOD, D), lambda b, t: (b, jnp.where(t < nct, 0, 1), 0, 0)),
            _const_spec(gpre.shape), _layer_spec(w_in_p, layer),
            tab, tab, tab, tab,
            _const_spec(gqk.shape), _const_spec(bd.shape), _const_spec(gcq.shape),
            _const_spec(gckv.shape), _layer_spec(wuq, layer), _layer_spec(wukv, layer)],
        out_specs=[row(w) for _, w, _ in outs],
        out_shape=[jax.ShapeDtypeStruct((B, N, w), dt) for _, w, dt in outs],
        compiler_params=_cparams(2),
        name="inproj",
    )(*stream, modsel, gpre, w_in_p, *tabs, gqk, bd, gcq, gckv, wuq, wukv)


def _attn_kernel(*refs, mixers):
    n_mix = len(mixers)
    jobs, v_aug = [], {}
    for i, (n_heads, q_w, kv_group, v_w) in enumerate(mixers):
        q_ref, k_ref, v_ref = refs[3 * i:3 * i + 3]
        ones = jnp.ones((k_ref.shape[1], LANES - v_w), BF16)
        for g in range(n_heads // kv_group):
            v_aug[i, g] = jnp.concatenate([v_ref[0, :, g * v_w:(g + 1) * v_w], ones], axis=1)
        jobs += [(i, hh, q_ref, k_ref, refs[3 * n_mix + i], q_w, kv_group, v_w) for hh in range(n_heads)]

    def scores(job):
        _, hh, q_ref, k_ref, _, q_w, kv_group, _ = job
        g = hh // kv_group
        return lax.dot_general(q_ref[0, :, hh * q_w:(hh + 1) * q_w], k_ref[0, :, g * q_w:(g + 1) * q_w],
                               (((1,), (1,)), ((), ())), preferred_element_type=F32)

    s_next = scores(jobs[0])
    for n, job in enumerate(jobs):
        i, hh, _, _, o_ref, _, kv_group, v_w = job
        s = s_next
        if n + 1 < len(jobs):
            s_next = scores(jobs[n + 1])
        p = jnp.exp2(s - jnp.max(s, axis=-1, keepdims=True))
        o = jnp.dot(p.astype(BF16), v_aug[i, hh // kv_group], preferred_element_type=F32)
        o_ref[0, :, hh * v_w:(hh + 1) * v_w] = (o[:, 0:v_w] / o[:, v_w:v_w + 1]).astype(o_ref.dtype)


def _attention(qkv, mixers, n_ctx, need_ctx):
    B, N, _ = qkv[0][0].shape
    n_lat = N - n_ctx
    ows = [n_heads * v_w for n_heads, _, _, v_w in mixers]
    kern = functools.partial(_attn_kernel, mixers=tuple(mixers))
    flat = [a for trio in qkv for a in trio]
    keys = lambda a, n: pl.BlockSpec((1, n, a.shape[2]), lambda b, t: (b, 0, 0))
    tq = ATTN_LATENT_TILE
    q_rows = lambda q: pl.BlockSpec((pl.Element(1), pl.Element(tq), pl.Element(q.shape[2])),
                                    lambda b, t: (b, pl.multiple_of(n_ctx + t * tq, ROW_TILE), 0))
    o_lat = pl.pallas_call(
        kern,
        grid=(B, n_lat // tq),
        in_specs=[s for q, k, v in qkv for s in (q_rows(q), keys(k, N), keys(v, N))],
        out_specs=[pl.BlockSpec((1, tq, ow), lambda b, t: (b, t, 0)) for ow in ows],
        out_shape=[jax.ShapeDtypeStruct((B, n_lat, ow), BF16) for ow in ows],
        compiler_params=_cparams(2),
        name="attn",
    )(*flat)
    if not need_ctx:
        return [(o,) for o in o_lat]
    tq = ROW_TILE
    rows = lambda w: pl.BlockSpec((1, tq, w), lambda b, t: (b, t, 0))
    o_ctx = pl.pallas_call(
        kern,
        grid=(B, n_ctx // tq),
        in_specs=[s for q, k, v in qkv for s in (rows(q.shape[2]), keys(k, n_ctx), keys(v, n_ctx))],
        out_specs=[rows(ow) for ow in ows],
        out_shape=[jax.ShapeDtypeStruct((B, n_ctx, ow), BF16) for ow in ows],
        compiler_params=_cparams(2),
        name="attn_ctx",
    )(*flat)
    return list(zip(o_ctx, o_lat))


def _gla_kernel(q_ref, k_ref, v_ref, misc_ref, gate_ref, wal_ref, bal_ref, gout_ref, o_ref,
                la_scr, acc_scr, s_scr, *, n_ctx):
    n_rows = q_ref.shape[1]
    ck, blk = B_CHUNK, GLA_BLOCK
    cpb = blk // ck
    nb, ncb = n_rows // blk, n_ctx // blk
    kw = B_HEADS * B_KEY_DIM

    def decay_rows(i, carry):
        r0 = pl.multiple_of(i * blk, blk)
        z = jnp.dot(misc_ref[0, pl.ds(r0, blk), :], wal_ref[...], preferred_element_type=F32) + bal_ref[...]
        la_scr[pl.ds(r0, blk), :] = (jnp.minimum(z, 0.0) - jnp.log1p(jnp.exp(-jnp.abs(z)))) * (1.0 / B_GATE_NORM)
        return carry
    lax.fori_loop(0, nb, decay_rows, 0)

    ri = lax.broadcasted_iota(jnp.int32, (blk, blk), 0)
    ci = lax.broadcasted_iota(jnp.int32, (blk, blk), 1)
    same_chunk = (ri // ck) == (ci // ck)

    heads = range(B_HEADS)
    ks = [slice(hh * B_KEY_DIM, (hh + 1) * B_KEY_DIM) for hh in heads]
    vs = [slice(hh * B_VAL_DIM, (hh + 1) * B_VAL_DIM) for hh in heads]

    def front(ib, backward):
        rows = pl.ds(pl.multiple_of(ib * blk, blk), blk)
        keep = same_chunk & ((ci >= ri) if backward else (ci <= ri))
        la = la_scr[rows, kw:2 * kw] if backward else la_scr[rows, 0:kw]
        tri = jnp.where(keep, 1.0, 0.0).astype(BF16)
        la_hi = la.astype(BF16)
        rem = la - la_hi.astype(F32)
        la_mid = rem.astype(BF16)
        la_lo = (rem - la_mid.astype(F32)).astype(BF16)
        b = (jnp.dot(tri, la_hi, preferred_element_type=F32)
             + jnp.dot(tri, la_mid, preferred_element_type=F32)
             + jnp.dot(tri, la_lo, preferred_element_type=F32))
        last = 0 if backward else ck - 1
        bl = jnp.concatenate([jnp.broadcast_to(b[j * ck + last:j * ck + last + 1, :], (ck, kw))
                              for j in range(cpb)], axis=0)
        qd = q_ref[0, rows, :] * jnp.exp(b)
        k = k_ref[0, rows, :]
        kd = k * jnp.exp(-b)
        ke_t = (k * jnp.exp(bl - b)).T
        g_t = jnp.exp(bl.T)
        v = v_ref[0, rows, :]
        attn = [lax.dot_general(qd[:, ks[hh]].astype(BF16), kd[:, ks[hh]].astype(BF16),
                                (((1,), (1,)), ((), ())), preferred_element_type=F32) for hh in heads]
        u_all = [jnp.dot(jnp.where(same_chunk, jnp.concatenate([ke_t[ks[hh], :]] * cpb, axis=0), 0.0).astype(BF16),
                         v[:, vs[hh]], preferred_element_type=F32) for hh in heads]
        o = [jnp.dot(jnp.where(keep, attn[hh], 0.0).astype(BF16), v[:, vs[hh]], preferred_element_type=F32)
             for hh in heads]
        qd4 = [jnp.where(same_chunk, jnp.concatenate([qd[:, ks[hh]]] * cpb, axis=1), 0.0).astype(BF16)
               for hh in heads]
        return g_t, u_all, o, qd4

    def back(fr, ib, backward):
        g_t, u_all, o, qd4 = fr
        rows = pl.ds(pl.multiple_of(ib * blk, blk), blk)
        order = range(cpb - 1, -1, -1) if backward else range(cpb)
        d = int(backward)
        s_stack = []
        for hh in heads:
            s = s_scr[d, hh]
            s_enter = [None] * cpb
            for j in order:
                s_enter[j] = s.astype(BF16)
                s = g_t[ks[hh], j * ck:j * ck + 1] * s + u_all[hh][j * ck:(j + 1) * ck, :]
            s_scr[d, hh] = s
            s_stack.append(jnp.concatenate(s_enter, axis=0))
        for hh in heads:
            tot = o[hh] + jnp.dot(qd4[hh], s_stack[hh], preferred_element_type=F32)
            if not backward:
                acc_scr[rows, vs[hh]] = tot
            else:
                tot = _rms(acc_scr[rows, vs[hh]] + tot, gout_ref[...])
                gt = gate_ref[0, rows, vs[hh]]
                o_ref[0, rows, vs[hh]] = (tot * (gt / (1.0 + jnp.exp(-gt)))).astype(o_ref.dtype)

    s_scr[...] = jnp.zeros_like(s_scr)
    def fwd(i, carry):
        back(front(i, False), i, False)
        return carry
    lax.fori_loop(0, nb, fwd, 0)

    def bwd(i, carry):
        ib = jnp.where(i < ncb, ncb - 1 - i, nb - 1 - (i - ncb))
        back(front(ib, True), ib, True)
        return carry
    lax.fori_loop(0, nb, bwd, 0)


def _gla(qb, kb, vb, misc, gb, wal, bal, gout, n_ctx):
    B, N, _ = qb.shape
    vw = B_HEADS * B_VAL_DIM
    blk = lambda w: pl.BlockSpec((1, N, w), lambda b: (b, 0, 0))
    return pl.pallas_call(
        functools.partial(_gla_kernel, n_ctx=n_ctx),
        grid=(B,),
        in_specs=[blk(qb.shape[2]), blk(kb.shape[2]), blk(vw), blk(LANES), blk(vw),
                  _const_spec(wal.shape), _const_spec(bal.shape), _const_spec(gout.shape)],
        out_specs=blk(vw),
        out_shape=jax.ShapeDtypeStruct((B, N, vw), BF16),
        scratch_shapes=[pltpu.VMEM((N, 2 * B_HEADS * B_KEY_DIM), F32),
                        pltpu.VMEM((N, vw), F32),
                        pltpu.VMEM((2, B_HEADS, B_KEY_DIM, B_VAL_DIM), F32)],
        compiler_params=_cparams(1),
        name="gla",
    )(qb, kb, vb, misc, gb, wal, bal, gout)


def _post_kernel(*refs, n_stream, n_attn, nct, t0):
    x_refs, refs = refs[:n_stream], refs[n_stream:]
    mod_ref, refs = refs[0], refs[1:]
    oa_refs, ob_ref, oc_refs = refs[:n_attn], refs[n_attn], refs[n_attn + 1:2 * n_attn + 1]
    (gates_ref, wbr_ref, wout_ref, gpost_ref, gpre2_ref,
     w1_ref, w2_ref, gpost2_ref, o_ref) = refs[2 * n_attn + 1:]
    d = o_ref.shape[2]
    m = mod_ref[0, 0]
    tr = o_ref.shape[1]
    halves = [slice(i * tr // POST_SPLIT, (i + 1) * tr // POST_SPLIT) for i in range(POST_SPLIT)]
    branches = (_stream_tile(oa_refs, nct, t0), ob_ref[0], _stream_tile(oc_refs, nct, t0))
    x = _stream_tile(x_refs, nct, t0)
    acc = []
    for r in halves:
        tot = None
        for z, branch in enumerate(branches):
            proj = jnp.dot(branch[r], wbr_ref[z], preferred_element_type=F32)
            term = gates_ref[0, r, z * d:(z + 1) * d].astype(F32) * proj
            tot = term if tot is None else tot + term
        acc.append(tot.astype(BF16))
    y = [jnp.dot(a, wout_ref[...], preferred_element_type=F32) for a in acc]
    x1 = [x[r] + m[2:3] * _rms(yh, gpost_ref[...]) for r, yh in zip(halves, y)]
    h2 = [(_rms(xh, gpre2_ref[...]) * (1.0 + m[4:5]) + m[3:4]).astype(BF16) for xh in x1]
    u = [jnp.dot(h, w1_ref[...], preferred_element_type=F32) for h in h2]
    a = [jnp.square(jnp.maximum(uh, 0.0)).astype(BF16) for uh in u]
    f = [jnp.dot(ah, w2_ref[...], preferred_element_type=F32) for ah in a]
    for r, xh, fh in zip(halves, x1, f):
        o_ref[0, r, :] = xh + m[5:6] * _rms(fh, gpost2_ref[...])


def _post(stream, modsel, oa, ob, oc, gates, wbr, wout, gpost, gpre2, w1, w2, gpost2, n_ctx, need_ctx, layer):
    B, N, _ = ob.shape
    D = stream[-1].shape[2]
    tr = ROW_TILE
    nct = n_ctx // tr
    t0 = 0 if need_ctx else nct
    specs = lambda arrs: _stream_specs(arrs, tr, n_ctx, N, t0)
    mod = pl.BlockSpec((1, 1, N_MOD, D), lambda b, t: (b, jnp.where(t + t0 < nct, 0, 1), 0, 0))
    consts = (wbr, wout, gpost, gpre2, w1, w2, gpost2)
    const_specs = [_layer_spec(w, layer) if w.ndim > 2 else _const_spec(w.shape) for w in consts]
    assert len(oa) == len(oc)
    return pl.pallas_call(
        functools.partial(_post_kernel, n_stream=len(stream), n_attn=len(oa), nct=nct, t0=t0),
        grid=(B, N // tr - t0),
        in_specs=specs(stream) + [mod] + specs(oa) + specs((ob,)) + specs(oc) + specs((gates,)) + const_specs,
        out_specs=pl.BlockSpec((1, tr, D), lambda b, t: (b, t, 0)),
        out_shape=jax.ShapeDtypeStruct((B, N - t0 * tr, D), F32),
        compiler_params=_cparams(2),
        name="post",
    )(*stream, modsel, *oa, ob, *oc, gates, *consts)


def _pairs_split(w, n_heads):
    lead, hd = w.shape[:-1], w.shape[-1] // n_heads
    return w.reshape(lead + (n_heads, hd // 2, 2)).swapaxes(-1, -2).reshape(lead + (n_heads * hd,))


def _w_in_layout(w_in):
    L, D, _ = w_in.shape
    z32 = jnp.zeros((L, D, 32), w_in.dtype)
    misc = jnp.concatenate([w_in[..., 1792:1824], z32, _pairs_split(w_in[..., 2976:3008], 1), z32], axis=-1)
    parts = [_pairs_split(w_in[..., 0:512], A_HEADS), _pairs_split(w_in[..., 512:640], A_KV_HEADS),
             w_in[..., 640:1792], misc, w_in[..., 1824:2976], w_in[..., 3008:]]
    out = jnp.concatenate(parts, axis=-1).astype(BF16)
    assert out.shape[-1] == _W_IN_P
    return out


def _w_uq_layout(w_uq):
    L, R, _ = w_uq.shape
    w = w_uq.reshape(L, R, C_HEADS, C_NOPE + C_ROPE)
    rope = _pairs_split(w[..., C_NOPE:], 1)
    pad = jnp.zeros((L, R, C_HEADS, C_HEAD_PAD - C_NOPE - C_ROPE), w.dtype)
    return jnp.concatenate([w[..., :C_NOPE], rope, pad], axis=-1).reshape(L, R, C_HEADS * C_HEAD_PAD).astype(BF16)


def _w_ukv_layout(w_ukv):
    L, R, _ = w_ukv.shape
    w = w_ukv.reshape(L, R, C_HEADS, C_NOPE + C_VDIM)
    pad = jnp.zeros((L, R, C_HEADS, C_HEAD_PAD - C_NOPE), w.dtype)
    k = jnp.concatenate([w[..., :C_NOPE], pad], axis=-1).reshape(L, R, C_HEADS * C_HEAD_PAD)
    v = w[..., C_NOPE:].reshape(L, R, C_HEADS * C_VDIM)
    return jnp.concatenate([k, v], axis=-1).astype(BF16)


def _w_alpha_layout(w_alpha):
    L, _, r, kw = w_alpha.shape
    z = jnp.zeros((L, r, kw), w_alpha.dtype)
    top = jnp.concatenate([w_alpha[:, 0], z], axis=-1)
    bot = jnp.concatenate([z, w_alpha[:, 1]], axis=-1)
    return jnp.concatenate([top, bot, jnp.zeros((L, LANES - 2 * r, 2 * kw), w_alpha.dtype)], axis=1).astype(BF16)


def _rope_angles(n_lat, rot_dim):
    rows = n_lat // GRID_W
    row = jnp.repeat(jnp.arange(rows, dtype=F32), GRID_W)
    col = jnp.tile(jnp.arange(GRID_W, dtype=F32), rows)
    n_freq = rot_dim // 4
    inv_freq = ROPE_THETA ** (-jnp.arange(n_freq, dtype=F32) / n_freq)
    ang = jnp.concatenate([row[:, None] * inv_freq, col[:, None] * inv_freq], axis=-1)
    return jnp.cos(ang), jnp.sin(ang)


def _rope_tables(n_ctx, n_lat):
    cos, sin = _rope_angles(n_lat, A_HEAD_DIM)
    cos_a = jnp.concatenate([cos] * 4, axis=-1)
    sin_a = jnp.concatenate([-sin, sin, -sin, sin], axis=-1)
    cos, sin = _rope_angles(n_lat, C_ROPE)
    lo, hi = _MISC_ROPE_LO, LANES - _MISC_ROPE_LO - C_ROPE
    cos_c = jnp.concatenate([jnp.ones((n_lat, lo), F32), cos, cos, jnp.ones((n_lat, hi), F32)], axis=-1)
    sin_c = jnp.concatenate([jnp.zeros((n_lat, lo), F32), -sin, sin, jnp.zeros((n_lat, hi), F32)], axis=-1)
    ident = lambda t, v: jnp.concatenate([jnp.full((n_ctx, LANES), v, F32), t], axis=0)
    return ident(cos_a, 1.0), ident(sin_a, 0.0), ident(cos_c, 1.0), ident(sin_c, 0.0)


def kernel(x, c, ctx, c_ctx, w_ada, b_ada, g_pre_attn, g_post_attn, g_pre_mlp, g_post_mlp, w_in, a_g_q, a_g_k,
           b_w_alpha, b_b_alpha, b_g_out, c_g_q, c_g_kv, c_w_uq, c_w_ukv, w_branch, w_out, w_mlp_in, w_mlp_out):
    B, n_lat, D = x.shape
    n_ctx = ctx.shape[1]
    L = w_ada.shape[0]
    assert n_ctx % ROW_TILE == 0 and n_lat % ROW_TILE == 0
    assert n_ctx % GLA_BLOCK == 0 and n_lat % GLA_BLOCK == 0 and GLA_BLOCK % B_CHUNK == 0
    assert n_lat % ATTN_LATENT_TILE == 0

    n_vec = -(-(B + 1) // 8) * 8
    cvec = jnp.concatenate([c, c_ctx[None], jnp.zeros((n_vec - B - 1, D), F32)], axis=0)
    mods = _ada(cvec, w_ada, b_ada)
    mod_lat = mods[:, :B].reshape(L, B, 1, N_MOD, D)
    mod_ctx = jnp.broadcast_to(mods[:, B].reshape(L, 1, 1, N_MOD, D), (L, B, 1, N_MOD, D))
    modsel = jnp.concatenate([mod_ctx, mod_lat], axis=2)

    tabs = _rope_tables(n_ctx, n_lat)
    w_in_p = _w_in_layout(w_in)
    gqk = jnp.concatenate([jnp.tile(_pairs_split(a_g_q, 1) * (A_HEAD_DIM ** -0.5 * LOG2E), (1, A_HEADS)),
                           jnp.tile(_pairs_split(a_g_k, 1), (1, A_KV_HEADS))], axis=-1)
    heads_per_bd = 256 // A_HEAD_DIM
    bd = jnp.kron(jnp.eye(heads_per_bd, dtype=F32), jnp.ones((A_HEAD_DIM, A_HEAD_DIM), F32)).astype(BF16)
    wuq, wukv = _w_uq_layout(c_w_uq), _w_ukv_layout(c_w_ukv)
    wal = _w_alpha_layout(b_w_alpha)
    bal = b_b_alpha.reshape(L, 1, 2 * B_HEADS * B_KEY_DIM)
    wbr, wout = w_branch.astype(BF16), w_out.astype(BF16)
    w1, w2 = w_mlp_in.astype(BF16), w_mlp_out.astype(BF16)
    vec = lambda g, l: g[l][None, :]

    stream = (ctx, x)
    for l in range(L):
        need_ctx = l < L - 1
        qa, ka, va, qb, kb, vb, misc, gb, qc, kc, vc, gates = _inproj(
            stream, modsel[l], vec(g_pre_attn, l), w_in_p, tabs, vec(gqk, l), bd,
            vec(c_g_q, l), vec(c_g_kv, l), wuq, wukv, n_ctx, l)
        oa, oc = _attention([(qa, ka, va), (qc, kc, vc)],
                            [(A_HEADS, A_HEAD_DIM, A_GROUP, A_HEAD_DIM), (C_HEADS, C_HEAD_PAD, 1, C_VDIM)],
                            n_ctx, need_ctx)
        ob = _gla(qb, kb, vb, misc, gb, wal[l], bal[l], vec(b_g_out, l), n_ctx)
        stream = (_post(stream, modsel[l], oa, ob, oc, gates, wbr, wout, vec(g_post_attn, l),
                        vec(g_pre_mlp, l), w1, w2, vec(g_post_mlp, l), n_ctx, need_ctx, l),)
    return stream[0]
```

```python
import functools

import jax
import jax.numpy as jnp
from jax import lax
from jax.experimental import pallas as pl
from jax.experimental.pallas import tpu as pltpu

EPS = 1e-6
ROPE_THETA = 10000.0
GRID_W = 64
N_MOD = 6
N_BRANCH = 3
A_HEADS, A_KV_HEADS, A_HEAD_DIM = 8, 2, 64
A_GROUP = A_HEADS // A_KV_HEADS
B_HEADS, B_KEY_DIM, B_VAL_DIM, B_GATE_RANK, B_GATE_NORM, B_CHUNK = 4, 64, 128, 16, 16.0, 64
C_HEADS, C_NOPE, C_ROPE, C_VDIM, C_Q_RANK, C_KV_RANK = 8, 64, 32, 64, 384, 256
BRANCH_W = 512

LANES = 128
V7X_VMEM_BYTES = 64 * 1024 * 1024
VMEM_LIMIT = V7X_VMEM_BYTES - 8 * 1024 * 1024

ROW_TILE = 256
GLA_BLOCK = 256
ATTN_LATENT_TILE = 512
POST_SPLIT = 2
C_HEAD_PAD = LANES

_QA, _KA, _VA = 0, 512, 640
_QB, _KB, _VB, _MISC, _GB = 768, 1024, 1280, 1792, 1920
_CQ, _CKV, _GATES, _W_IN_P = 2432, 2816, 3072, 6144
_MISC_ROPE_LO = 64

F32 = jnp.float32
BF16 = jnp.bfloat16
LOG2E = 1.4426950408889634


def _cparams(n_grid_axes):
    return pltpu.CompilerParams(dimension_semantics=("arbitrary",) * n_grid_axes,
                                vmem_limit_bytes=VMEM_LIMIT)


def _const_spec(shape):
    nd = len(shape)
    return pl.BlockSpec(shape, lambda *_: (0,) * nd, pipeline_mode=pl.Buffered(1))


def _layer_spec(w, layer):
    nd = w.ndim
    return pl.BlockSpec((None,) + w.shape[1:], lambda *_: (layer,) + (0,) * (nd - 1),
                        pipeline_mode=pl.Buffered(1))


def _stream_specs(arrs, tr, n_ctx, n_total, t0=0):
    nct, d = n_ctx // tr, arrs[0].shape[2]
    if len(arrs) == 2:
        return [pl.BlockSpec((1, tr, d), lambda b, t: (b, jnp.minimum(t + t0, nct - 1), 0)),
                pl.BlockSpec((1, tr, d), lambda b, t: (b, jnp.maximum(t + t0 - nct, 0), 0))]
    if arrs[0].shape[1] == n_total:
        return [pl.BlockSpec((1, tr, d), lambda b, t: (b, t + t0, 0))]
    assert arrs[0].shape[1] == n_total - n_ctx and t0 == nct
    return [pl.BlockSpec((1, tr, d), lambda b, t: (b, t, 0))]


def _stream_tile(refs, nct, t0=0):
    if len(refs) == 1:
        return refs[0][0]
    is_ctx = pl.program_id(1) + t0 < nct
    return jnp.where(is_ctx, refs[0][0], refs[1][0])


def _rms(x, g):
    return x * lax.rsqrt(jnp.mean(x * x, axis=-1, keepdims=True) + EPS) * g


def _swap_halves(x, half):
    lane = lax.broadcasted_iota(jnp.int32, x.shape, 1)
    first = (lane % (2 * half)) < half
    return jnp.where(first, pltpu.roll(x, LANES - half, 1), pltpu.roll(x, half, 1))


def _rope_slab(y, cos, sin, half):
    return y * cos + _swap_halves(y, half) * sin


def _ada_kernel(c_ref, w_ref, b_ref, o_ref):
    cv = c_ref[...]
    s = (cv / (1.0 + jnp.exp(-cv))).astype(BF16)
    o_ref[0] = jnp.dot(s, w_ref[0].astype(BF16), preferred_element_type=F32) + b_ref[0]


def _ada(cvec, w_ada, b_ada):
    L, D, W = w_ada.shape
    R = cvec.shape[0]
    tn = 1024
    return pl.pallas_call(
        _ada_kernel,
        grid=(L, W // tn),
        in_specs=[pl.BlockSpec((R, D), lambda l, j: (0, 0)),
                  pl.BlockSpec((1, D, tn), lambda l, j: (l, 0, j)),
                  pl.BlockSpec((1, 1, tn), lambda l, j: (l, 0, j))],
        out_specs=pl.BlockSpec((1, R, tn), lambda l, j: (l, 0, j)),
        out_shape=jax.ShapeDtypeStruct((L, R, W), F32),
        compiler_params=_cparams(2),
        name="ada",
    )(cvec, w_ada, b_ada.reshape(L, 1, W))


def _inproj_kernel(*refs, n_stream, nct):
    x_refs = refs[:n_stream]
    (mod_ref, gpre_ref, w_ref, cosa_ref, sina_ref, cosc_ref, sinc_ref,
     gqk_ref, bd_ref, gcq_ref, gckv_ref, wuq_ref, wukv_ref,
     qa_ref, ka_ref, va_ref, qb_ref, kb_ref, vb_ref, misc_ref, gb_ref,
     qc_ref, kc_ref, vc_ref, gates_ref) = refs[n_stream:]
    x = _stream_tile(x_refs, nct)
    m = mod_ref[0, 0]
    h = (_rms(x, gpre_ref[...]) * (1.0 + m[1:2]) + m[0:1]).astype(BF16)

    def proj(lo, hi):
        return jnp.dot(h, w_ref[:, lo:hi], preferred_element_type=F32)

    zg = proj(_GATES, _W_IN_P)
    gates_ref[0] = (1.0 / (1.0 + jnp.exp(-zg))).astype(BF16)

    za = proj(_QA, _QB)
    cosa, sina = cosa_ref[...], sina_ref[...]

    def headnorm_rope(z, gain, out_ref, out_lo):
        w = z.shape[1]
        sq = z * z
        sq_hi = sq.astype(BF16)
        sq_lo = (sq - sq_hi.astype(F32)).astype(BF16)
        bd = bd_ref[:w, :w]
        ss = (jnp.dot(sq_hi, bd, preferred_element_type=F32)
              + jnp.dot(sq_lo, bd, preferred_element_type=F32))
        y = z * lax.rsqrt(ss * (1.0 / A_HEAD_DIM) + EPS) * gain
        for s in range(w // LANES):
            ys = _rope_slab(y[:, s * LANES:(s + 1) * LANES], cosa, sina, A_HEAD_DIM // 2)
            out_ref[0, :, out_lo + s * LANES:out_lo + (s + 1) * LANES] = ys.astype(out_ref.dtype)

    headnorm_rope(za[:, 0:256], gqk_ref[:, 0:256], qa_ref, 0)
    headnorm_rope(za[:, 256:512], gqk_ref[:, 256:512], qa_ref, 256)
    headnorm_rope(za[:, 512:640], gqk_ref[:, 512:640], ka_ref, 0)
    va_ref[0] = za[:, 640:768].astype(BF16)

    zb = proj(_QB, _CQ)
    qb_ref[0] = zb[:, 0:256] * (B_KEY_DIM ** -0.5)
    kb_ref[0] = zb[:, 256:512]
    vb_ref[0] = zb[:, 512:1024].astype(BF16)
    cosc, sinc = cosc_ref[...], sinc_ref[...]
    misc = _rope_slab(zb[:, 1024:1152], cosc, sinc, C_ROPE // 2)
    misc_ref[0] = misc.astype(BF16)
    gb_ref[0] = zb[:, 1152:1664]

    zc = proj(_CQ, _GATES)
    cqn = _rms(zc[:, 0:C_Q_RANK], gcq_ref[...]).astype(BF16)
    qup = jnp.dot(cqn, wuq_ref[...], preferred_element_type=F32)
    scale_c = (C_NOPE + C_ROPE) ** -0.5 * LOG2E
    for hh in range(C_HEADS):
        sl = slice(hh * C_HEAD_PAD, (hh + 1) * C_HEAD_PAD)
        qc_ref[0, :, sl] = (_rope_slab(qup[:, sl], cosc, sinc, C_ROPE // 2) * scale_c).astype(BF16)
    ckvn = _rms(zc[:, C_Q_RANK:C_Q_RANK + C_KV_RANK], gckv_ref[...]).astype(BF16)
    kvup = jnp.dot(ckvn, wukv_ref[...], preferred_element_type=F32)
    lane = lax.broadcasted_iota(jnp.int32, misc.shape, 1)
    krope = jnp.where((lane >= _MISC_ROPE_LO) & (lane < _MISC_ROPE_LO + C_ROPE), misc, 0.0)
    for hh in range(C_HEADS):
        sl = slice(hh * C_HEAD_PAD, (hh + 1) * C_HEAD_PAD)
        kc_ref[0, :, sl] = (kvup[:, sl] + krope).astype(BF16)
    vc_ref[0] = kvup[:, C_HEADS * C_HEAD_PAD:].astype(BF16)


def _inproj(stream, modsel, gpre, w_in_p, tabs, gqk, bd, gcq, gckv, wuq, wukv, n_ctx, layer):
    B, D = stream[-1].shape[0], stream[-1].shape[2]
    N = tabs[0].shape[0]
    tr = ROW_TILE
    nct = n_ctx // tr
    row =lambda w: pl.BlockSpec((1, tr, w), lambda b, t: (b, t, 0))
    tab = pl.BlockSpec((tr, LANES), lambda b, t: (t, 0))
    outs = [("qa", 512, BF16), ("ka", 128, BF16), ("va", 128, BF16),
            ("qb", 256, F32), ("kb", 256, F32), ("vb", 512, BF16), ("misc", 128, BF16), ("gb", 512, F32),
            ("qc", C_HEADS * C_HEAD_PAD, BF16), ("kc", C_HEADS * C_HEAD_PAD, BF16),
            ("vc", C_HEADS * C_VDIM, BF16), ("gates", N_BRANCH * D, BF16)]
    return pl.pallas_call(
        functools.partial(_inproj_kernel, n_stream=len(stream), nct=nct),
        grid=(B, N // tr),
        in_specs=_stream_specs(stream, tr, n_ctx, N) + [
            pl.BlockSpec((1, 1, N_MOD, D), lambda b, t: (b, jnp.where(t < nct, 0, 1), 0, 0)),
            _const_spec(gpre.shape), _layer_spec(w_in_p, layer),
            tab, tab, tab, tab,
            _const_spec(gqk.shape), _const_spec(bd.shape), _const_spec(gcq.shape),
            _const_spec(gckv.shape), _layer_spec(wuq, layer), _layer_spec(wukv, layer)],
        out_specs=[row(w) for _, w, _ in outs],
        out_shape=[jax.ShapeDtypeStruct((B, N, w), dt) for _, w, dt in outs],
        compiler_params=_cparams(2),
        name="inproj",
    )(*stream, modsel, gpre, w_in_p, *tabs, gqk, bd, gcq, gckv, wuq, wukv)


def _attn_kernel(*refs, mixers, ahead):
    n_mix = len(mixers)
    jobs, v_aug = [], {}
    for i, (n_heads, q_w, kv_group, v_w) in enumerate(mixers):
        q_ref, k_ref, v_ref = refs[3 * i:3 * i + 3]
        ones = jnp.ones((k_ref.shape[1], LANES - v_w), BF16)
        for g in range(n_heads // kv_group):
            v_aug[i, g] = jnp.concatenate([v_ref[0, :, g * v_w:(g + 1) * v_w], ones], axis=1)
        jobs += [(i, hh, q_ref, k_ref, refs[3 * n_mix + i], q_w, kv_group, v_w) for hh in range(n_heads)]

    def scores(job):
        _, hh, q_ref, k_ref, _, q_w, kv_group, _ = job
        g = hh // kv_group
        return lax.dot_general(q_ref[0, :, hh * q_w:(hh + 1) * q_w], k_ref[0, :, g * q_w:(g + 1) * q_w],
                               (((1,), (1,)), ((), ())), preferred_element_type=F32)

    ahead = min(ahead, len(jobs))
    queue = [scores(job) for job in jobs[:ahead]]
    for n, job in enumerate(jobs):
        i, hh, _, _, o_ref, _, kv_group, v_w = job
        s = queue.pop(0)
        if n + ahead < len(jobs):
            queue.append(scores(jobs[n + ahead]))
        p = jnp.exp2(s - jnp.max(s, axis=-1, keepdims=True))
        o = jnp.dot(p.astype(BF16), v_aug[i, hh // kv_group], preferred_element_type=F32)
        o_ref[0, :, hh * v_w:(hh + 1) * v_w] = (o[:, 0:v_w] / o[:, v_w:v_w + 1]).astype(o_ref.dtype)


def _attention(qkv, mixers, n_ctx, need_ctx):
    B, N, _ = qkv[0][0].shape
    n_lat = N - n_ctx
    ows = [n_heads * v_w for n_heads, _, _, v_w in mixers]
    n_jobs = sum(n_heads for n_heads, _, _, _ in mixers)
    kern = functools.partial(_attn_kernel, mixers=tuple(mixers), ahead=1)
    kern_ctx = functools.partial(_attn_kernel, mixers=tuple(mixers), ahead=n_jobs)
    flat =[a for trio in qkv for a in trio]
    keys = lambda a, n: pl.BlockSpec((1, n, a.shape[2]), lambda b, t: (b, 0, 0))
    tq = ATTN_LATENT_TILE
    q_rows = lambda q: pl.BlockSpec((pl.Element(1), pl.Element(tq), pl.Element(q.shape[2])),
                                    lambda b, t: (b, pl.multiple_of(n_ctx + t * tq, ROW_TILE), 0))
    o_lat = pl.pallas_call(
        kern,
        grid=(B, n_lat // tq),
        in_specs=[s for q, k, v in qkv for s in (q_rows(q), keys(k, N), keys(v, N))],
        out_specs=[pl.BlockSpec((1, tq, ow), lambda b, t: (b, t, 0)) for ow in ows],
        out_shape=[jax.ShapeDtypeStruct((B, n_lat, ow), BF16) for ow in ows],
        compiler_params=_cparams(2),
        name="attn",
    )(*flat)
    if not need_ctx:
        return [(o,) for o in o_lat]
    tq = ROW_TILE
    rows = lambda w: pl.BlockSpec((1, tq, w), lambda b, t: (b, t, 0))
    o_ctx = pl.pallas_call(
        kern_ctx,
        grid=(B, n_ctx // tq),
        in_specs=[s for q, k, v in qkv for s in (rows(q.shape[2]), keys(k, n_ctx), keys(v, n_ctx))],
        out_specs=[rows(ow) for ow in ows],
        out_shape=[jax.ShapeDtypeStruct((B, n_ctx, ow), BF16) for ow in ows],
        compiler_params=_cparams(2),
        name="attn_ctx",
    )(*flat)
    return list(zip(o_ctx, o_lat))


def _gla_kernel(q_ref, k_ref, v_ref, misc_ref, gate_ref, wal_ref, bal_ref, gout_ref, o_ref,
                la_scr, acc_scr, s_scr, *, n_ctx):
    n_rows = q_ref.shape[1]
    ck, blk = B_CHUNK, GLA_BLOCK
    cpb = blk // ck
    nb, ncb = n_rows // blk, n_ctx // blk
    kw = B_HEADS * B_KEY_DIM

    def decay_rows(i, carry):
        r0 = pl.multiple_of(i * blk, blk)
        z = jnp.dot(misc_ref[0, pl.ds(r0, blk), :], wal_ref[...], preferred_element_type=F32) + bal_ref[...]
        la_scr[pl.ds(r0, blk), :] = (jnp.minimum(z, 0.0) - jnp.log1p(jnp.exp(-jnp.abs(z)))) * (1.0 / B_GATE_NORM)
        return carry
    lax.fori_loop(0, nb, decay_rows, 0)

    ri = lax.broadcasted_iota(jnp.int32, (blk, blk), 0)
    ci = lax.broadcasted_iota(jnp.int32, (blk, blk), 1)
    same_chunk = (ri // ck) == (ci // ck)

    heads = range(B_HEADS)
    ks = [slice(hh * B_KEY_DIM, (hh + 1) * B_KEY_DIM) for hh in heads]
    vs = [slice(hh * B_VAL_DIM, (hh + 1) * B_VAL_DIM) for hh in heads]

    def front(ib, backward):
        rows = pl.ds(pl.multiple_of(ib * blk, blk), blk)
        keep = same_chunk & ((ci >= ri) if backward else (ci <= ri))
        la = la_scr[rows, kw:2 * kw] if backward else la_scr[rows, 0:kw]
        tri = jnp.where(keep, 1.0, 0.0).astype(BF16)
        la_hi = la.astype(BF16)
        rem = la - la_hi.astype(F32)
        la_mid = rem.astype(BF16)
        la_lo = (rem - la_mid.astype(F32)).astype(BF16)
        b = (jnp.dot(tri, la_hi, preferred_element_type=F32)
             + jnp.dot(tri, la_mid, preferred_element_type=F32)
             + jnp.dot(tri, la_lo, preferred_element_type=F32))
        last = 0 if backward else ck - 1
        bl = jnp.concatenate([jnp.broadcast_to(b[j * ck + last:j * ck + last + 1, :], (ck, kw))
                              for j in range(cpb)], axis=0)
        qd = q_ref[0, rows, :] * jnp.exp(b)
        k = k_ref[0, rows, :]
        kd = k * jnp.exp(-b)
        ke_t = (k * jnp.exp(bl - b)).T
        g_t = jnp.exp(bl.T)
        v = v_ref[0, rows, :]
        attn = [lax.dot_general(qd[:, ks[hh]].astype(BF16), kd[:, ks[hh]].astype(BF16),
                                (((1,), (1,)), ((), ())), preferred_element_type=F32) for hh in heads]
        u_all = [jnp.dot(jnp.where(same_chunk, jnp.concatenate([ke_t[ks[hh], :]] * cpb, axis=0), 0.0).astype(BF16),
                         v[:, vs[hh]], preferred_element_type=F32) for hh in heads]
        o = [jnp.dot(jnp.where(keep, attn[hh], 0.0).astype(BF16), v[:, vs[hh]], preferred_element_type=F32)
             for hh in heads]
        qd4 = [jnp.where(same_chunk, jnp.concatenate([qd[:, ks[hh]]] * cpb, axis=1), 0.0).astype(BF16)
               for hh in heads]
        return g_t, u_all, o, qd4

    def back(fr, ib, backward):
        g_t, u_all, o, qd4 = fr
        rows = pl.ds(pl.multiple_of(ib * blk, blk), blk)
        order = range(cpb - 1, -1, -1) if backward else range(cpb)
        d = int(backward)
        s_stack = []
        for hh in heads:
            s = s_scr[d, hh]
            s_enter = [None] * cpb
            for j in order:
                s_enter[j] = s.astype(BF16)
                s = g_t[ks[hh], j * ck:j * ck + 1] * s + u_all[hh][j * ck:(j + 1) * ck, :]
            s_scr[d, hh] = s
            s_stack.append(jnp.concatenate(s_enter, axis=0))
        for hh in heads:
            tot = o[hh] + jnp.dot(qd4[hh], s_stack[hh], preferred_element_type=F32)
            if not backward:
                acc_scr[rows, vs[hh]] = tot
            else:
                tot = _rms(acc_scr[rows, vs[hh]] + tot, gout_ref[...])
                gt = gate_ref[0, rows, vs[hh]]
                o_ref[0, rows, vs[hh]] = (tot * (gt / (1.0 + jnp.exp(-gt)))).astype(o_ref.dtype)

    s_scr[...] = jnp.zeros_like(s_scr)
    def fwd(i, carry):
        back(front(i, False), i, False)
        return carry
    lax.fori_loop(0, nb, fwd, 0)

    def bwd(i, carry):
        ib = jnp.where(i < ncb, ncb - 1 - i, nb - 1 - (i - ncb))
        back(front(ib, True), ib, True)
        return carry
    lax.fori_loop(0, nb, bwd, 0)


def _gla(qb, kb, vb, misc, gb, wal, bal, gout, n_ctx):
    B, N, _ = qb.shape
    vw = B_HEADS * B_VAL_DIM
    blk = lambda w: pl.BlockSpec((1, N, w), lambda b: (b, 0, 0))
    return pl.pallas_call(
        functools.partial(_gla_kernel, n_ctx=n_ctx),
        grid=(B,),
        in_specs=[blk(qb.shape[2]), blk(kb.shape[2]), blk(vw), blk(LANES), blk(vw),
                  _const_spec(wal.shape), _const_spec(bal.shape), _const_spec(gout.shape)],
        out_specs=blk(vw),
        out_shape=jax.ShapeDtypeStruct((B, N, vw), BF16),
        scratch_shapes=[pltpu.VMEM((N, 2 * B_HEADS * B_KEY_DIM), F32),
                        pltpu.VMEM((N, vw), F32),
                        pltpu.VMEM((2, B_HEADS, B_KEY_DIM, B_VAL_DIM), F32)],
        compiler_params=_cparams(1),
        name="gla",
    )(qb, kb, vb, misc, gb, wal, bal, gout)


def _post_kernel(*refs, n_stream, n_attn, nct, t0):
    x_refs, refs = refs[:n_stream], refs[n_stream:]
    mod_ref, refs = refs[0], refs[1:]
    oa_refs, ob_ref, oc_refs = refs[:n_attn], refs[n_attn], refs[n_attn + 1:2 * n_attn + 1]
    (gates_ref, wbr_ref, wout_ref, gpost_ref, gpre2_ref,
     w1_ref, w2_ref, gpost2_ref, o_ref) = refs[2 * n_attn + 1:]
    d = o_ref.shape[2]
    m = mod_ref[0, 0]
    tr = o_ref.shape[1]
    halves = [slice(i * tr // POST_SPLIT, (i + 1) * tr // POST_SPLIT) for i in range(POST_SPLIT)]
    branches = (_stream_tile(oa_refs, nct, t0), ob_ref[0], _stream_tile(oc_refs, nct, t0))
    x = _stream_tile(x_refs, nct, t0)
    acc = []
    for r in halves:
        tot = None
        for z, branch in enumerate(branches):
            proj = jnp.dot(branch[r], wbr_ref[z], preferred_element_type=F32)
            term = gates_ref[0, r, z * d:(z + 1) * d].astype(F32) * proj
            tot = term if tot is None else tot + term
        acc.append(tot.astype(BF16))
    y = [jnp.dot(a, wout_ref[...], preferred_element_type=F32) for a in acc]
    x1 = [x[r] + m[2:3] * _rms(yh, gpost_ref[...]) for r, yh in zip(halves, y)]
    h2 = [(_rms(xh, gpre2_ref[...]) * (1.0 + m[4:5]) + m[3:4]).astype(BF16) for xh in x1]
    u = [jnp.dot(h, w1_ref[...], preferred_element_type=F32) for h in h2]
    a = [jnp.square(jnp.maximum(uh, 0.0)).astype(BF16) for uh in u]
    f = [jnp.dot(ah, w2_ref[...], preferred_element_type=F32) for ah in a]
    for r, xh, fh in zip(halves, x1, f):
        o_ref[0, r, :] = xh + m[5:6] * _rms(fh, gpost2_ref[...])


def _post(stream, modsel, oa, ob, oc, gates, wbr, wout, gpost, gpre2, w1, w2, gpost2, n_ctx, need_ctx, layer):
    B, N, _ = ob.shape
    D = stream[-1].shape[2]
    tr = ROW_TILE
    nct = n_ctx // tr
    t0 = 0 if need_ctx else nct
    specs = lambda arrs: _stream_specs(arrs, tr, n_ctx, N, t0)
    mod = pl.BlockSpec((1, 1, N_MOD, D), lambda b, t: (b, jnp.where(t + t0 < nct, 0, 1), 0, 0))
    consts = (wbr, wout, gpost, gpre2, w1, w2, gpost2)
    const_specs = [_layer_spec(w, layer) if w.ndim > 2 else _const_spec(w.shape) for w in consts]
    assert len(oa) == len(oc)
    return pl.pallas_call(
        functools.partial(_post_kernel, n_stream=len(stream), n_attn=len(oa), nct=nct, t0=t0),
        grid=(B, N // tr - t0),
        in_specs=specs(stream) + [mod] + specs(oa) + specs((ob,)) + specs(oc) + specs((gates,)) + const_specs,
        out_specs=pl.BlockSpec((1, tr, D), lambda b, t: (b, t, 0)),
        out_shape=jax.ShapeDtypeStruct((B, N - t0 * tr, D), F32),
        compiler_params=_cparams(2),
        name="post",
    )(*stream, modsel, *oa, ob, *oc, gates, *consts)


def _pairs_split(w, n_heads):
    lead, hd = w.shape[:-1], w.shape[-1] // n_heads
    return w.reshape(lead + (n_heads, hd // 2, 2)).swapaxes(-1, -2).reshape(lead + (n_heads * hd,))


def _w_in_layout(w_in):
    L, D, _ = w_in.shape
    w_in = w_in.astype(BF16)
    z32 = jnp.zeros((L, D, 32), w_in.dtype)
    misc = jnp.concatenate([w_in[..., 1792:1824], z32, _pairs_split(w_in[..., 2976:3008], 1), z32], axis=-1)
    parts = [_pairs_split(w_in[..., 0:512], A_HEADS), _pairs_split(w_in[..., 512:640], A_KV_HEADS),
             w_in[..., 640:1792], misc, w_in[..., 1824:2976], w_in[..., 3008:]]
    out = jnp.concatenate(parts, axis=-1).astype(BF16)
    assert out.shape[-1] == _W_IN_P
    return out


def _w_uq_layout(w_uq):
    L, R, _ = w_uq.shape
    w = w_uq.reshape(L, R, C_HEADS, C_NOPE + C_ROPE)
    rope = _pairs_split(w[..., C_NOPE:], 1)
    pad = jnp.zeros((L, R, C_HEADS, C_HEAD_PAD - C_NOPE - C_ROPE), w.dtype)
    return jnp.concatenate([w[..., :C_NOPE], rope, pad], axis=-1).reshape(L, R, C_HEADS * C_HEAD_PAD).astype(BF16)


def _w_ukv_layout(w_ukv):
    L, R, _ = w_ukv.shape
    w = w_ukv.reshape(L, R, C_HEADS, C_NOPE + C_VDIM)
    pad = jnp.zeros((L, R, C_HEADS, C_HEAD_PAD - C_NOPE), w.dtype)
    k = jnp.concatenate([w[..., :C_NOPE], pad], axis=-1).reshape(L, R, C_HEADS * C_HEAD_PAD)
    v = w[..., C_NOPE:].reshape(L, R, C_HEADS * C_VDIM)
    return jnp.concatenate([k, v], axis=-1).astype(BF16)


def _w_alpha_layout(w_alpha):
    L, _, r, kw = w_alpha.shape
    z = jnp.zeros((L, r, kw), w_alpha.dtype)
    top = jnp.concatenate([w_alpha[:, 0], z], axis=-1)
    bot = jnp.concatenate([z, w_alpha[:, 1]], axis=-1)
    return jnp.concatenate([top, bot, jnp.zeros((L, LANES - 2 * r, 2 * kw), w_alpha.dtype)], axis=1).astype(BF16)


def _rope_angles(n_lat, rot_dim):
    rows = n_lat // GRID_W
    row = jnp.repeat(jnp.arange(rows, dtype=F32), GRID_W)
    col = jnp.tile(jnp.arange(GRID_W, dtype=F32), rows)
    n_freq = rot_dim // 4
    inv_freq = ROPE_THETA ** (-jnp.arange(n_freq, dtype=F32) / n_freq)
    ang = jnp.concatenate([row[:, None] * inv_freq, col[:, None] * inv_freq], axis=-1)
    return jnp.cos(ang), jnp.sin(ang)


def _rope_tables(n_ctx, n_lat):
    cos, sin = _rope_angles(n_lat, A_HEAD_DIM)
    cos_a = jnp.concatenate([cos] * 4, axis=-1)
    sin_a = jnp.concatenate([-sin, sin, -sin, sin], axis=-1)
    cos, sin = _rope_angles(n_lat, C_ROPE)
    lo, hi = _MISC_ROPE_LO, LANES - _MISC_ROPE_LO - C_ROPE
    cos_c = jnp.concatenate([jnp.ones((n_lat, lo), F32), cos, cos, jnp.ones((n_lat, hi), F32)], axis=-1)
    sin_c = jnp.concatenate([jnp.zeros((n_lat, lo), F32), -sin, sin, jnp.zeros((n_lat, hi), F32)], axis=-1)
    ident = lambda t, v: jnp.concatenate([jnp.full((n_ctx, LANES), v, F32), t], axis=0)
    return ident(cos_a, 1.0), ident(sin_a, 0.0), ident(cos_c, 1.0), ident(sin_c, 0.0)


def kernel(x, c, ctx, c_ctx, w_ada, b_ada, g_pre_attn, g_post_attn, g_pre_mlp, g_post_mlp, w_in, a_g_q, a_g_k,
           b_w_alpha, b_b_alpha, b_g_out, c_g_q, c_g_kv, c_w_uq, c_w_ukv, w_branch, w_out, w_mlp_in, w_mlp_out):
    B, n_lat, D = x.shape
    n_ctx = ctx.shape[1]
    L = w_ada.shape[0]
    assert n_ctx % ROW_TILE == 0 and n_lat % ROW_TILE == 0
    assert n_ctx % GLA_BLOCK == 0 and n_lat % GLA_BLOCK == 0 and GLA_BLOCK % B_CHUNK == 0
    assert n_lat % ATTN_LATENT_TILE == 0

    n_vec = -(-(B + 1) // 8) * 8
    cvec = jnp.concatenate([c, c_ctx[None], jnp.zeros((n_vec - B - 1, D), F32)], axis=0)
    mods = _ada(cvec, w_ada, b_ada)
    mod_lat = mods[:, :B].reshape(L, B, 1, N_MOD, D)
    mod_ctx = jnp.broadcast_to(mods[:, B].reshape(L, 1, 1, N_MOD, D), (L, B, 1, N_MOD, D))
    modsel = jnp.concatenate([mod_ctx, mod_lat], axis=2)

    tabs = _rope_tables(n_ctx, n_lat)
    w_in_p = _w_in_layout(w_in)
    gqk = jnp.concatenate([jnp.tile(_pairs_split(a_g_q, 1) * (A_HEAD_DIM ** -0.5 * LOG2E), (1, A_HEADS)),
                           jnp.tile(_pairs_split(a_g_k, 1), (1, A_KV_HEADS))], axis=-1)
    heads_per_bd = 256 // A_HEAD_DIM
    bd = jnp.kron(jnp.eye(heads_per_bd, dtype=F32), jnp.ones((A_HEAD_DIM, A_HEAD_DIM), F32)).astype(BF16)
    wuq, wukv = _w_uq_layout(c_w_uq), _w_ukv_layout(c_w_ukv)
    wal = _w_alpha_layout(b_w_alpha)
    bal = b_b_alpha.reshape(L, 1, 2 * B_HEADS * B_KEY_DIM)
    wbr, wout = w_branch.astype(BF16), w_out.astype(BF16)
    w1, w2 = w_mlp_in.astype(BF16), w_mlp_out.astype(BF16)
    vec = lambda g, l: g[l][None, :]

    stream = (ctx, x)
    for l in range(L):
        need_ctx = l < L - 1
        qa, ka, va, qb, kb, vb, misc, gb, qc, kc, vc, gates = _inproj(
            stream, modsel[l], vec(g_pre_attn, l), w_in_p, tabs, vec(gqk, l), bd,
            vec(c_g_q, l), vec(c_g_kv, l), wuq, wukv, n_ctx, l)
        oa, oc = _attention([(qa, ka, va), (qc, kc, vc)],
                            [(A_HEADS, A_HEAD_DIM, A_GROUP, A_HEAD_DIM), (C_HEADS, C_HEAD_PAD, 1, C_VDIM)],
                            n_ctx, need_ctx)
        ob = _gla(qb, kb, vb, misc, gb, wal[l], bal[l], vec(b_g_out, l), n_ctx)
        stream = (_post(stream, modsel[l], oa, ob, oc, gates, wbr, wout, vec(g_post_attn, l),
                        vec(g_pre_mlp, l), w1, w2, vec(g_post_mlp, l), n_ctx, need_ctx, l),)
    return stream[0]
```

```python
import functools

import jax
import jax.numpy as jnp
from jax import lax
from jax.experimental import pallas as pl
from jax.experimental.pallas import tpu as pltpu

EPS = 1e-6
ROPE_THETA = 10000.0
GRID_W = 64
N_MOD = 6
N_BRANCH = 3
A_HEADS, A_KV_HEADS, A_HEAD_DIM = 8, 2, 64
A_GROUP = A_HEADS // A_KV_HEADS
B_HEADS, B_KEY_DIM, B_VAL_DIM, B_GATE_RANK, B_GATE_NORM, B_CHUNK = 4, 64, 128, 16, 16.0, 64
C_HEADS, C_NOPE, C_ROPE, C_VDIM, C_Q_RANK, C_KV_RANK = 8, 64, 32, 64, 384, 256
BRANCH_W = 512

LANES = 128
V7X_VMEM_BYTES = 64 * 1024 * 1024
VMEM_LIMIT = V7X_VMEM_BYTES - 8 * 1024 * 1024

ROW_TILE = 256
GLA_BLOCK = 256
ATTN_LATENT_TILE = 512
POST_SPLIT = 2
C_HEAD_PAD = LANES

_QA, _KA, _VA = 0, 512, 640
_QB, _KB, _VB, _MISC, _GB = 768, 1024, 1280, 1792, 1920
_CQ, _CKV, _GATES, _W_IN_P = 2432, 2816, 3072, 6144
_MISC_ROPE_LO = 64

F32 = jnp.float32
BF16 = jnp.bfloat16
LOG2E = 1.4426950408889634


def _cparams(n_grid_axes):
    return pltpu.CompilerParams(dimension_semantics=("arbitrary",) * n_grid_axes,
                                vmem_limit_bytes=VMEM_LIMIT)


def _const_spec(shape):
    nd = len(shape)
    return pl.BlockSpec(shape, lambda *_: (0,) * nd, pipeline_mode=pl.Buffered(1))


def _layer_spec(w, layer):
    nd = w.ndim
    return pl.BlockSpec((None,) + w.shape[1:], lambda *_: (layer,) + (0,) * (nd - 1),
                        pipeline_mode=pl.Buffered(1))


def _stream_specs(arrs, tr, n_ctx, n_total, t0=0):
    nct, d = n_ctx // tr, arrs[0].shape[2]
    if len(arrs) == 2:
        return [pl.BlockSpec((1, tr, d), lambda b, t: (b, jnp.minimum(t + t0, nct - 1), 0)),
                pl.BlockSpec((1, tr, d), lambda b, t: (b, jnp.maximum(t + t0 - nct, 0), 0))]
    if arrs[0].shape[1] == n_total:
        return [pl.BlockSpec((1, tr, d), lambda b, t: (b, t + t0, 0))]
    assert arrs[0].shape[1] == n_total - n_ctx and t0 == nct
    return [pl.BlockSpec((1, tr, d), lambda b, t: (b, t, 0))]


def _stream_tile(refs, nct, t0=0):
    if len(refs) == 1:
        return refs[0][0]
    is_ctx = pl.program_id(1) + t0 < nct
    return jnp.where(is_ctx, refs[0][0], refs[1][0])


def _rms(x, g):
    return x * lax.rsqrt(jnp.mean(x * x, axis=-1, keepdims=True) + EPS) * g


def _swap_halves(x, half):
    lane = lax.broadcasted_iota(jnp.int32, x.shape, 1)
    first = (lane % (2 * half)) < half
    return jnp.where(first, pltpu.roll(x, LANES - half, 1), pltpu.roll(x, half, 1))


def _rope_slab(y, cos, sin, half):
    return y * cos + _swap_halves(y, half) * sin


def _ada_kernel(c_ref, w_ref, b_ref, o_ref):
    cv = c_ref[...]
    s = (cv / (1.0 + jnp.exp(-cv))).astype(BF16)
    o_ref[0] = jnp.dot(s, w_ref[0].astype(BF16), preferred_element_type=F32) + b_ref[0]


def _ada(cvec, w_ada, b_ada):
    L, D, W = w_ada.shape
    R = cvec.shape[0]
    tn = 1024
    return pl.pallas_call(
        _ada_kernel,
        grid=(L, W // tn),
        in_specs=[pl.BlockSpec((R, D), lambda l, j: (0, 0)),
                  pl.BlockSpec((1, D, tn), lambda l, j: (l, 0, j)),
                  pl.BlockSpec((1, 1, tn), lambda l, j: (l, 0, j))],
        out_specs=pl.BlockSpec((1, R, tn), lambda l, j: (l, 0, j)),
        out_shape=jax.ShapeDtypeStruct((L, R, W), F32),
        compiler_params=_cparams(2),
        name="ada",
    )(cvec, w_ada, b_ada.reshape(L, 1, W))


def _inproj_kernel(*refs, n_stream, nct):
    x_refs = refs[:n_stream]
    (mod_ref, gpre_ref, w_ref, cosa_ref, sina_ref, cosc_ref, sinc_ref,
     gqk_ref, bd_ref, gcq_ref, gckv_ref, wuq_ref, wukv_ref,
     qa_ref, ka_ref, va_ref, qb_ref, kb_ref, vb_ref, misc_ref, gb_ref,
     qc_ref, kc_ref, vc_ref, gates_ref) = refs[n_stream:]
    x = _stream_tile(x_refs, nct)
    m = mod_ref[0, 0]
    h = (_rms(x, gpre_ref[...]) * (1.0 + m[1:2]) + m[0:1]).astype(BF16)

    def proj(lo, hi):
        return jnp.dot(h, w_ref[:, lo:hi], preferred_element_type=F32)


    zb = proj(_QB, _CQ)
    qb_ref[0] = zb[:, 0:256] * (B_KEY_DIM ** -0.5)
    kb_ref[0] = zb[:, 256:512]
    vb_ref[0] = zb[:, 512:1024].astype(BF16)
    cosc, sinc = cosc_ref[...], sinc_ref[...]
    misc = _rope_slab(zb[:, 1024:1152], cosc, sinc, C_ROPE // 2)
    misc_ref[0] = misc.astype(BF16)
    gb_ref[0] = zb[:, 1152:1664]

    zc = proj(_CQ, _GATES)
    cqn = _rms(zc[:, 0:C_Q_RANK], gcq_ref[...]).astype(BF16)
    qup = jnp.dot(cqn, wuq_ref[...], preferred_element_type=F32)
    scale_c = (C_NOPE + C_ROPE) ** -0.5 * LOG2E
    for hh in range(C_HEADS):
        sl = slice(hh * C_HEAD_PAD, (hh + 1) * C_HEAD_PAD)
        qc_ref[0, :, sl] = (_rope_slab(qup[:, sl], cosc, sinc, C_ROPE // 2) * scale_c).astype(BF16)
    ckvn = _rms(zc[:, C_Q_RANK:C_Q_RANK + C_KV_RANK], gckv_ref[...]).astype(BF16)
    kvup = jnp.dot(ckvn, wukv_ref[...], preferred_element_type=F32)
    lane = lax.broadcasted_iota(jnp.int32, misc.shape, 1)
    krope = jnp.where((lane >= _MISC_ROPE_LO) & (lane < _MISC_ROPE_LO + C_ROPE), misc, 0.0)
    for hh in range(C_HEADS):
        sl = slice(hh * C_HEAD_PAD, (hh + 1) * C_HEAD_PAD)
        kc_ref[0, :, sl] = (kvup[:, sl] + krope).astype(BF16)
    vc_ref[0] = kvup[:, C_HEADS * C_HEAD_PAD:].astype(BF16)

    za = proj(_QA, _QB)

    zg = proj(_GATES, _W_IN_P)
    gates_ref[0] = (1.0 / (1.0 + jnp.exp(-zg))).astype(BF16)

    cosa, sina = cosa_ref[...], sina_ref[...]

    def headnorm_rope(z, gain, out_ref, out_lo):
        w = z.shape[1]
        sq = z * z
        sq_hi = sq.astype(BF16)
        sq_lo = (sq - sq_hi.astype(F32)).astype(BF16)
        bd = bd_ref[:w, :w]
        ss = (jnp.dot(sq_hi, bd, preferred_element_type=F32)
              + jnp.dot(sq_lo, bd, preferred_element_type=F32))
        y = z * lax.rsqrt(ss * (1.0 / A_HEAD_DIM) + EPS) * gain
        for s in range(w // LANES):
            ys = _rope_slab(y[:, s * LANES:(s + 1) * LANES], cosa, sina, A_HEAD_DIM // 2)
            out_ref[0, :, out_lo + s * LANES:out_lo + (s + 1) * LANES] = ys.astype(out_ref.dtype)

    headnorm_rope(za[:, 0:256], gqk_ref[:, 0:256], qa_ref, 0)
    headnorm_rope(za[:, 256:512], gqk_ref[:, 256:512], qa_ref, 256)
    headnorm_rope(za[:, 512:640], gqk_ref[:, 512:640], ka_ref, 0)
    va_ref[0] = za[:, 640:768].astype(BF16)


def _inproj(stream, modsel, gpre, w_in_p, tabs, gqk, bd, gcq, gckv, wuq, wukv, n_ctx, layer):
    B, D = stream[-1].shape[0], stream[-1].shape[2]
    N = tabs[0].shape[0]
    tr = ROW_TILE
    nct = n_ctx // tr
    row =lambda w: pl.BlockSpec((1, tr, w), lambda b, t: (b, t, 0))
    tab = pl.BlockSpec((tr, LANES), lambda b, t: (t, 0))
    outs = [("qa", 512, BF16), ("ka", 128, BF16), ("va", 128, BF16),
            ("qb", 256, F32), ("kb", 256, F32), ("vb", 512, BF16), ("misc", 128, BF16), ("gb", 512, F32),
            ("qc", C_HEADS * C_HEAD_PAD, BF16), ("kc", C_HEADS * C_HEAD_PAD, BF16),
            ("vc", C_HEADS * C_VDIM, BF16), ("gates", N_BRANCH * D, BF16)]
    return pl.pallas_call(
        functools.partial(_inproj_kernel, n_stream=len(stream), nct=nct),
        grid=(B, N // tr),
        in_specs=_stream_specs(stream, tr, n_ctx, N) + [
            pl.BlockSpec((1, 1, N_MOD, D), lambda b, t: (b, jnp.where(t < nct, 0, 1), 0, 0)),
            _const_spec(gpre.shape), _layer_spec(w_in_p, layer),
            tab, tab, tab, tab,
            _const_spec(gqk.shape), _const_spec(bd.shape), _const_spec(gcq.shape),
            _const_spec(gckv.shape), _layer_spec(wuq, layer), _layer_spec(wukv, layer)],
        out_specs=[row(w) for _, w, _ in outs],
        out_shape=[jax.ShapeDtypeStruct((B, N, w), dt) for _, w, dt in outs],
        compiler_params=_cparams(2),
        name="inproj",
    )(*stream, modsel, gpre, w_in_p, *tabs, gqk, bd, gcq, gckv, wuq, wukv)


def _attn_kernel(*refs, mixers, ahead):
    n_mix = len(mixers)
    jobs, v_aug = [], {}
    for i, (n_heads, q_w, kv_group, v_w) in enumerate(mixers):
        q_ref, k_ref, v_ref = refs[3 * i:3 * i + 3]
        ones = jnp.ones((k_ref.shape[1], LANES - v_w), BF16)
        for g in range(n_heads // kv_group):
            v_aug[i, g] = jnp.concatenate([v_ref[0, :, g * v_w:(g + 1) * v_w], ones], axis=1)
        jobs += [(i, hh, q_ref, k_ref, refs[3 * n_mix + i], q_w, kv_group, v_w) for hh in range(n_heads)]

    def scores(job):
        _, hh, q_ref, k_ref, _, q_w, kv_group, _ = job
        g = hh // kv_group
        return lax.dot_general(q_ref[0, :, hh * q_w:(hh + 1) * q_w], k_ref[0, :, g * q_w:(g + 1) * q_w],
                               (((1,), (1,)), ((), ())), preferred_element_type=F32)

    ahead = min(ahead, len(jobs))
    queue = [scores(job) for job in jobs[:ahead]]
    for n, job in enumerate(jobs):
        i, hh, _, _, o_ref, _, kv_group, v_w = job
        s = queue.pop(0)
        if n + ahead < len(jobs):
            queue.append(scores(jobs[n + ahead]))
        p = jnp.exp2(s - jnp.max(s, axis=-1, keepdims=True))
        o = jnp.dot(p.astype(BF16), v_aug[i, hh // kv_group], preferred_element_type=F32)
        o_ref[0, :, hh * v_w:(hh + 1) * v_w] = (o[:, 0:v_w] / o[:, v_w:v_w + 1]).astype(o_ref.dtype)


def _attention(qkv, mixers, n_ctx, need_ctx):
    B, N, _ = qkv[0][0].shape
    n_lat = N - n_ctx
    ows = [n_heads * v_w for n_heads, _, _, v_w in mixers]
    n_jobs = sum(n_heads for n_heads, _, _, _ in mixers)
    kern = functools.partial(_attn_kernel, mixers=tuple(mixers), ahead=1)
    kern_ctx = functools.partial(_attn_kernel, mixers=tuple(mixers), ahead=n_jobs)
    flat =[a for trio in qkv for a in trio]
    keys = lambda a, n: pl.BlockSpec((1, n, a.shape[2]), lambda b, t: (b, 0, 0))
    tq = ATTN_LATENT_TILE
    q_rows = lambda q: pl.BlockSpec((pl.Element(1), pl.Element(tq), pl.Element(q.shape[2])),
                                    lambda b, t: (b, pl.multiple_of(n_ctx + t * tq, ROW_TILE), 0))
    o_lat = pl.pallas_call(
        kern,
        grid=(B, n_lat // tq),
        in_specs=[s for q, k, v in qkv for s in (q_rows(q), keys(k, N), keys(v, N))],
        out_specs=[pl.BlockSpec((1, tq, ow), lambda b, t: (b, t, 0)) for ow in ows],
        out_shape=[jax.ShapeDtypeStruct((B, n_lat, ow), BF16) for ow in ows],
        compiler_params=_cparams(2),
        name="attn",
    )(*flat)
    if not need_ctx:
        return [(o,) for o in o_lat]
    tq = ROW_TILE
    rows = lambda w: pl.BlockSpec((1, tq, w), lambda b, t: (b, t, 0))
    o_ctx = pl.pallas_call(
        kern_ctx,
        grid=(B, n_ctx // tq),
        in_specs=[s for q, k, v in qkv for s in (rows(q.shape[2]), keys(k, n_ctx), keys(v, n_ctx))],
        out_specs=[rows(ow) for ow in ows],
        out_shape=[jax.ShapeDtypeStruct((B, n_ctx, ow), BF16) for ow in ows],
        compiler_params=_cparams(2),
        name="attn_ctx",
    )(*flat)
    return list(zip(o_ctx, o_lat))


def _gla_kernel(q_ref, k_ref, v_ref, misc_ref, gate_ref, wal_ref, bal_ref, gout_ref, o_ref,
                la_scr, acc_scr, s_scr, *, n_ctx):
    n_rows = q_ref.shape[1]
    ck, blk = B_CHUNK, GLA_BLOCK
    cpb = blk // ck
    nb, ncb = n_rows // blk, n_ctx // blk
    kw = B_HEADS * B_KEY_DIM

    def decay_rows(i, carry):
        r0 = pl.multiple_of(i * blk, blk)
        z = jnp.dot(misc_ref[0, pl.ds(r0, blk), :], wal_ref[...], preferred_element_type=F32) + bal_ref[...]
        la_scr[pl.ds(r0, blk), :] = (jnp.minimum(z, 0.0) - jnp.log1p(jnp.exp(-jnp.abs(z)))) * (1.0 / B_GATE_NORM)
        return carry
    lax.fori_loop(0, nb, decay_rows, 0)

    ri = lax.broadcasted_iota(jnp.int32, (blk, blk), 0)
    ci = lax.broadcasted_iota(jnp.int32, (blk, blk), 1)
    same_chunk = (ri // ck) == (ci // ck)

    heads = range(B_HEADS)
    ks = [slice(hh * B_KEY_DIM, (hh + 1) * B_KEY_DIM) for hh in heads]
    vs = [slice(hh * B_VAL_DIM, (hh + 1) * B_VAL_DIM) for hh in heads]

    def front(ib, backward):
        rows = pl.ds(pl.multiple_of(ib * blk, blk), blk)
        keep = same_chunk & ((ci >= ri) if backward else (ci <= ri))
        la = la_scr[rows, kw:2 * kw] if backward else la_scr[rows, 0:kw]
        tri = jnp.where(keep, 1.0, 0.0).astype(BF16)
        la_hi = la.astype(BF16)
        rem = la - la_hi.astype(F32)
        la_mid = rem.astype(BF16)
        la_lo = (rem - la_mid.astype(F32)).astype(BF16)
        b = (jnp.dot(tri, la_hi, preferred_element_type=F32)
             + jnp.dot(tri, la_mid, preferred_element_type=F32)
             + jnp.dot(tri, la_lo, preferred_element_type=F32))
        last = 0 if backward else ck - 1
        bl = jnp.concatenate([jnp.broadcast_to(b[j * ck + last:j * ck + last + 1, :], (ck, kw))
                              for j in range(cpb)], axis=0)
        qd = q_ref[0, rows, :] * jnp.exp(b)
        k = k_ref[0, rows, :]
        kd = k * jnp.exp(-b)
        ke_t = (k * jnp.exp(bl - b)).T
        g_t = jnp.exp(bl.T)
        v = v_ref[0, rows, :]
        attn = [lax.dot_general(qd[:, ks[hh]].astype(BF16), kd[:, ks[hh]].astype(BF16),
                                (((1,), (1,)), ((), ())), preferred_element_type=F32) for hh in heads]
        u_all = [jnp.dot(jnp.where(same_chunk, jnp.concatenate([ke_t[ks[hh], :]] * cpb, axis=0), 0.0).astype(BF16),
                         v[:, vs[hh]], preferred_element_type=F32) for hh in heads]
        o = [jnp.dot(jnp.where(keep, attn[hh], 0.0).astype(BF16), v[:, vs[hh]], preferred_element_type=F32)
             for hh in heads]
        qd4 = [jnp.where(same_chunk, jnp.concatenate([qd[:, ks[hh]]] * cpb, axis=1), 0.0).astype(BF16)
               for hh in heads]
        return g_t, u_all, o, qd4

    def back(fr, ib, backward):
        g_t, u_all, o, qd4 = fr
        rows = pl.ds(pl.multiple_of(ib * blk, blk), blk)
        order = range(cpb - 1, -1, -1) if backward else range(cpb)
        d = int(backward)
        s_stack = []
        for hh in heads:
            s = s_scr[d, hh]
            s_enter = [None] * cpb
            for j in order:
                s_enter[j] = s.astype(BF16)
                s = g_t[ks[hh], j * ck:j * ck + 1] * s + u_all[hh][j * ck:(j + 1) * ck, :]
            s_scr[d, hh] = s
            s_stack.append(jnp.concatenate(s_enter, axis=0))
        for hh in heads:
            tot = o[hh] + jnp.dot(qd4[hh], s_stack[hh], preferred_element_type=F32)
            if not backward:
                acc_scr[rows, vs[hh]] = tot
            else:
                tot = _rms(acc_scr[rows, vs[hh]] + tot, gout_ref[...])
                gt = gate_ref[0, rows, vs[hh]]
                o_ref[0, rows, vs[hh]] = (tot * (gt / (1.0 + jnp.exp(-gt)))).astype(o_ref.dtype)

    s_scr[...] = jnp.zeros_like(s_scr)
    def fwd(i, carry):
        back(front(i, False), i, False)
        return carry
    lax.fori_loop(0, nb, fwd, 0)

    def bwd(i, carry):
        ib = jnp.where(i < ncb, ncb - 1 - i, nb - 1 - (i - ncb))
        back(front(ib, True), ib, True)
        return carry
    lax.fori_loop(0, nb, bwd, 0)


def _gla(qb, kb, vb, misc, gb, wal, bal, gout, n_ctx):
    B, N, _ = qb.shape
    vw = B_HEADS * B_VAL_DIM
    blk = lambda w: pl.BlockSpec((1, N, w), lambda b: (b, 0, 0))
    return pl.pallas_call(
        functools.partial(_gla_kernel, n_ctx=n_ctx),
        grid=(B,),
        in_specs=[blk(qb.shape[2]), blk(kb.shape[2]), blk(vw), blk(LANES), blk(vw),
                  _const_spec(wal.shape), _const_spec(bal.shape), _const_spec(gout.shape)],
        out_specs=blk(vw),
        out_shape=jax.ShapeDtypeStruct((B, N, vw), BF16),
        scratch_shapes=[pltpu.VMEM((N, 2 * B_HEADS * B_KEY_DIM), F32),
                        pltpu.VMEM((N, vw), F32),
                        pltpu.VMEM((2, B_HEADS, B_KEY_DIM, B_VAL_DIM), F32)],
        compiler_params=_cparams(1),
        name="gla",
    )(qb, kb, vb, misc, gb, wal, bal, gout)


def _post_kernel(*refs, n_stream, n_attn, nct, t0):
    x_refs, refs = refs[:n_stream], refs[n_stream:]
    mod_ref, refs = refs[0], refs[1:]
    oa_refs, ob_ref, oc_refs = refs[:n_attn], refs[n_attn], refs[n_attn + 1:2 * n_attn + 1]
    (gates_ref, wbr_ref, wout_ref, gpost_ref, gpre2_ref,
     w1_ref, w2_ref, gpost2_ref, o_ref) = refs[2 * n_attn + 1:]
    d = o_ref.shape[2]
    m = mod_ref[0, 0]
    tr = o_ref.shape[1]
    halves = [slice(i * tr // POST_SPLIT, (i + 1) * tr // POST_SPLIT) for i in range(POST_SPLIT)]
    branches = (_stream_tile(oa_refs, nct, t0), ob_ref[0], _stream_tile(oc_refs, nct, t0))
    x = _stream_tile(x_refs, nct, t0)
    acc = []
    for r in halves:
        tot = None
        for z, branch in enumerate(branches):
            proj = jnp.dot(branch[r], wbr_ref[z], preferred_element_type=F32)
            term = gates_ref[0, r, z * d:(z + 1) * d].astype(F32) * proj
            tot = term if tot is None else tot + term
        acc.append(tot.astype(BF16))
    y = [jnp.dot(a, wout_ref[...], preferred_element_type=F32) for a in acc]
    x1 = [x[r] + m[2:3] * _rms(yh, gpost_ref[...]) for r, yh in zip(halves, y)]
    h2 = [(_rms(xh, gpre2_ref[...]) * (1.0 + m[4:5]) + m[3:4]).astype(BF16) for xh in x1]
    u = [jnp.dot(h, w1_ref[...], preferred_element_type=F32) for h in h2]
    a = [jnp.square(jnp.maximum(uh, 0.0)).astype(BF16) for uh in u]
    f = [jnp.dot(ah, w2_ref[...], preferred_element_type=F32) for ah in a]
    for r, xh, fh in zip(halves, x1, f):
        o_ref[0, r, :] = xh + m[5:6] * _rms(fh, gpost2_ref[...])


def _post(stream, modsel, oa, ob, oc, gates, wbr, wout, gpost, gpre2, w1, w2, gpost2, n_ctx, need_ctx, layer):
    B, N, _ = ob.shape
    D = stream[-1].shape[2]
    tr = ROW_TILE
    nct = n_ctx // tr
    t0 = 0 if need_ctx else nct
    specs = lambda arrs: _stream_specs(arrs, tr, n_ctx, N, t0)
    mod = pl.BlockSpec((1, 1, N_MOD, D), lambda b, t: (b, jnp.where(t + t0 < nct, 0, 1), 0, 0))
    consts = (wbr, wout, gpost, gpre2, w1, w2, gpost2)
    const_specs = [_layer_spec(w, layer) if w.ndim > 2 else _const_spec(w.shape) for w in consts]
    assert len(oa) == len(oc)
    return pl.pallas_call(
        functools.partial(_post_kernel, n_stream=len(stream), n_attn=len(oa), nct=nct, t0=t0),
        grid=(B, N // tr - t0),
        in_specs=specs(stream) + [mod] + specs(oa) + specs((ob,)) + specs(oc) + specs((gates,)) + const_specs,
        out_specs=pl.BlockSpec((1, tr, D), lambda b, t: (b, t, 0)),
        out_shape=jax.ShapeDtypeStruct((B, N - t0 * tr, D), F32),
        compiler_params=_cparams(2),
        name="post",
    )(*stream, modsel, *oa, ob, *oc, gates, *consts)


def _pairs_split(w, n_heads):
    lead, hd = w.shape[:-1], w.shape[-1] // n_heads
    return w.reshape(lead + (n_heads, hd // 2, 2)).swapaxes(-1, -2).reshape(lead + (n_heads * hd,))


def _w_in_layout(w_in):
    L, D, _ = w_in.shape
    w_in = w_in.astype(BF16)
    z32 = jnp.zeros((L, D, 32), w_in.dtype)
    misc = jnp.concatenate([w_in[..., 1792:1824], z32, _pairs_split(w_in[..., 2976:3008], 1), z32], axis=-1)
    parts = [_pairs_split(w_in[..., 0:512], A_HEADS), _pairs_split(w_in[..., 512:640], A_KV_HEADS),
             w_in[..., 640:1792], misc, w_in[..., 1824:2976], w_in[..., 3008:]]
    out = jnp.concatenate(parts, axis=-1).astype(BF16)
    assert out.shape[-1] == _W_IN_P
    return out


def _w_uq_layout(w_uq):
    L, R, _ = w_uq.shape
    w = w_uq.reshape(L, R, C_HEADS, C_NOPE + C_ROPE)
    rope = _pairs_split(w[..., C_NOPE:], 1)
    pad = jnp.zeros((L, R, C_HEADS, C_HEAD_PAD - C_NOPE - C_ROPE), w.dtype)
    return jnp.concatenate([w[..., :C_NOPE], rope, pad], axis=-1).reshape(L, R, C_HEADS * C_HEAD_PAD).astype(BF16)


def _w_ukv_layout(w_ukv):
    L, R, _ = w_ukv.shape
    w = w_ukv.reshape(L, R, C_HEADS, C_NOPE + C_VDIM)
    pad = jnp.zeros((L, R, C_HEADS, C_HEAD_PAD - C_NOPE), w.dtype)
    k = jnp.concatenate([w[..., :C_NOPE], pad], axis=-1).reshape(L, R, C_HEADS * C_HEAD_PAD)
    v = w[..., C_NOPE:].reshape(L, R, C_HEADS * C_VDIM)
    return jnp.concatenate([k, v], axis=-1).astype(BF16)


def _w_alpha_layout(w_alpha):
    L, _, r, kw = w_alpha.shape
    z = jnp.zeros((L, r, kw), w_alpha.dtype)
    top = jnp.concatenate([w_alpha[:, 0], z], axis=-1)
    bot = jnp.concatenate([z, w_alpha[:, 1]], axis=-1)
    return jnp.concatenate([top, bot, jnp.zeros((L, LANES - 2 * r, 2 * kw), w_alpha.dtype)], axis=1).astype(BF16)


def _rope_angles(n_lat, rot_dim):
    rows = n_lat // GRID_W
    row = jnp.repeat(jnp.arange(rows, dtype=F32), GRID_W)
    col = jnp.tile(jnp.arange(GRID_W, dtype=F32), rows)
    n_freq = rot_dim // 4
    inv_freq = ROPE_THETA ** (-jnp.arange(n_freq, dtype=F32) / n_freq)
    ang = jnp.concatenate([row[:, None] * inv_freq, col[:, None] * inv_freq], axis=-1)
    return jnp.cos(ang), jnp.sin(ang)


def _rope_tables(n_ctx, n_lat):
    cos, sin = _rope_angles(n_lat, A_HEAD_DIM)
    cos_a = jnp.concatenate([cos] * 4, axis=-1)
    sin_a = jnp.concatenate([-sin, sin, -sin, sin], axis=-1)
    cos, sin = _rope_angles(n_lat, C_ROPE)
    lo, hi = _MISC_ROPE_LO, LANES - _MISC_ROPE_LO - C_ROPE
    cos_c = jnp.concatenate([jnp.ones((n_lat, lo), F32), cos, cos, jnp.ones((n_lat, hi), F32)], axis=-1)
    sin_c = jnp.concatenate([jnp.zeros((n_lat, lo), F32), -sin, sin, jnp.zeros((n_lat, hi), F32)], axis=-1)
    ident = lambda t, v: jnp.concatenate([jnp.full((n_ctx, LANES), v, F32), t], axis=0)
    return ident(cos_a, 1.0), ident(sin_a, 0.0), ident(cos_c, 1.0), ident(sin_c, 0.0)


def kernel(x, c, ctx, c_ctx, w_ada, b_ada, g_pre_attn, g_post_attn, g_pre_mlp, g_post_mlp, w_in, a_g_q, a_g_k,
           b_w_alpha, b_b_alpha, b_g_out, c_g_q, c_g_kv, c_w_uq, c_w_ukv, w_branch, w_out, w_mlp_in, w_mlp_out):
    B, n_lat, D = x.shape
    n_ctx = ctx.shape[1]
    L = w_ada.shape[0]
    assert n_ctx % ROW_TILE == 0 and n_lat % ROW_TILE == 0
    assert n_ctx % GLA_BLOCK == 0 and n_lat % GLA_BLOCK == 0 and GLA_BLOCK % B_CHUNK == 0
    assert n_lat % ATTN_LATENT_TILE == 0

    n_vec = -(-(B + 1) // 8) * 8
    cvec = jnp.concatenate([c, c_ctx[None], jnp.zeros((n_vec - B - 1, D), F32)], axis=0)
    mods = _ada(cvec, w_ada, b_ada)
    mod_lat = mods[:, :B].reshape(L, B, 1, N_MOD, D)
    mod_ctx = jnp.broadcast_to(mods[:, B].reshape(L, 1, 1, N_MOD, D), (L, B, 1, N_MOD, D))
    modsel = jnp.concatenate([mod_ctx, mod_lat], axis=2)

    tabs = _rope_tables(n_ctx, n_lat)
    w_in_p = _w_in_layout(w_in)
    gqk = jnp.concatenate([jnp.tile(_pairs_split(a_g_q, 1) * (A_HEAD_DIM ** -0.5 * LOG2E), (1, A_HEADS)),
                           jnp.tile(_pairs_split(a_g_k, 1), (1, A_KV_HEADS))], axis=-1)
    heads_per_bd = 256 // A_HEAD_DIM
    bd = jnp.kron(jnp.eye(heads_per_bd, dtype=F32), jnp.ones((A_HEAD_DIM, A_HEAD_DIM), F32)).astype(BF16)
    wuq, wukv = _w_uq_layout(c_w_uq), _w_ukv_layout(c_w_ukv)
    wal = _w_alpha_layout(b_w_alpha)
    bal = b_b_alpha.reshape(L, 1, 2 * B_HEADS * B_KEY_DIM)
    wbr, wout = w_branch.astype(BF16), w_out.astype(BF16)
    w1, w2 = w_mlp_in.astype(BF16), w_mlp_out.astype(BF16)
    vec = lambda g, l: g[l][None, :]

    stream = (ctx, x)
    for l in range(L):
        need_ctx = l < L - 1
        qa, ka, va, qb, kb, vb, misc, gb, qc, kc, vc, gates = _inproj(
            stream, modsel[l], vec(g_pre_attn, l), w_in_p, tabs, vec(gqk, l), bd,
            vec(c_g_q, l), vec(c_g_kv, l), wuq, wukv, n_ctx, l)
        oa, oc = _attention([(qa, ka, va), (qc, kc, vc)],
                            [(A_HEADS, A_HEAD_DIM, A_GROUP, A_HEAD_DIM), (C_HEADS, C_HEAD_PAD, 1, C_VDIM)],
                            n_ctx, need_ctx)
        ob = _gla(qb, kb, vb, misc, gb, wal[l], bal[l], vec(b_g_out, l), n_ctx)
        stream = (_post(stream, modsel[l], oa, ob, oc, gates, wbr, wout, vec(g_post_attn, l),
                        vec(g_pre_mlp, l), w1, w2, vec(g_post_mlp, l), n_ctx, need_ctx, l),)
    return stream[0]
```

```python
import functools

import jax
import jax.numpy as jnp
from jax import lax
from jax.experimental import pallas as pl
from jax.experimental.pallas import tpu as pltpu

EPS = 1e-6
ROPE_THETA = 10000.0
GRID_W = 64
N_MOD = 6
N_BRANCH = 3
A_HEADS, A_KV_HEADS, A_HEAD_DIM = 8, 2, 64
A_GROUP = A_HEADS // A_KV_HEADS
B_HEADS, B_KEY_DIM, B_VAL_DIM, B_GATE_RANK, B_GATE_NORM, B_CHUNK = 4, 64, 128, 16, 16.0, 64
C_HEADS, C_NOPE, C_ROPE, C_VDIM, C_Q_RANK, C_KV_RANK = 8, 64, 32, 64, 384, 256
BRANCH_W = 512

LANES = 128
V7X_VMEM_BYTES = 64 * 1024 * 1024
VMEM_LIMIT = V7X_VMEM_BYTES - 8 * 1024 * 1024

ROW_TILE = 256
GLA_BLOCK = 256
ATTN_LATENT_TILE = 512
POST_SPLIT = 2
C_HEAD_PAD = LANES

_QA, _KA, _VA = 0, 512, 640
_QB, _KB, _VB, _MISC, _GB = 768, 1024, 1280, 1792, 1920
_CQ, _CKV, _GATES, _W_IN_P = 2432, 2816, 3072, 6144
_MISC_ROPE_LO = 64

F32 = jnp.float32
BF16 = jnp.bfloat16
LOG2E = 1.4426950408889634


def _cparams(n_grid_axes):
    return pltpu.CompilerParams(dimension_semantics=("arbitrary",) * n_grid_axes,
                                vmem_limit_bytes=VMEM_LIMIT)


def _const_spec(shape):
    nd = len(shape)
    return pl.BlockSpec(shape, lambda *_: (0,) * nd, pipeline_mode=pl.Buffered(1))


def _layer_spec(w, layer):
    nd = w.ndim
    return pl.BlockSpec((None,) + w.shape[1:], lambda *_: (layer,) + (0,) * (nd - 1),
                        pipeline_mode=pl.Buffered(1))


def _stream_specs(arrs, tr, n_ctx, n_total, t0=0):
    nct, d = n_ctx // tr, arrs[0].shape[2]
    if len(arrs) == 2:
        return [pl.BlockSpec((1, tr, d), lambda b, t: (b, jnp.minimum(t + t0, nct - 1), 0)),
                pl.BlockSpec((1, tr, d), lambda b, t: (b, jnp.maximum(t + t0 - nct, 0), 0))]
    if arrs[0].shape[1] == n_total:
        return [pl.BlockSpec((1, tr, d), lambda b, t: (b, t + t0, 0))]
    assert arrs[0].shape[1] == n_total - n_ctx and t0 == nct
    return [pl.BlockSpec((1, tr, d), lambda b, t: (b, t, 0))]


def _stream_tile(refs, nct, t0=0):
    if len(refs) == 1:
        return refs[0][0]
    is_ctx = pl.program_id(1) + t0 < nct
    return jnp.where(is_ctx, refs[0][0], refs[1][0])


def _rms(x, g):
    return x * lax.rsqrt(jnp.mean(x * x, axis=-1, keepdims=True) + EPS) * g


def _swap_halves(x, half):
    lane = lax.broadcasted_iota(jnp.int32, x.shape, 1)
    first = (lane % (2 * half)) < half
    return jnp.where(first, pltpu.roll(x, LANES - half, 1), pltpu.roll(x, half, 1))


def _rope_slab(y, cos, sin, half):
    return y * cos + _swap_halves(y, half) * sin


def _ada_kernel(c_ref, w_ref, b_ref, o_ref):
    cv = c_ref[...]
    s = (cv / (1.0 + jnp.exp(-cv))).astype(BF16)
    o_ref[0] = jnp.dot(s, w_ref[0].astype(BF16), preferred_element_type=F32) + b_ref[0]


def _ada(cvec, w_ada, b_ada):
    L, D, W = w_ada.shape
    R = cvec.shape[0]
    tn = 1024
    return pl.pallas_call(
        _ada_kernel,
        grid=(L, W // tn),
        in_specs=[pl.BlockSpec((R, D), lambda l, j: (0, 0)),
                  pl.BlockSpec((1, D, tn), lambda l, j: (l, 0, j)),
                  pl.BlockSpec((1, 1, tn), lambda l, j: (l, 0, j))],
        out_specs=pl.BlockSpec((1, R, tn), lambda l, j: (l, 0, j)),
        out_shape=jax.ShapeDtypeStruct((L, R, W), F32),
        compiler_params=_cparams(2),
        name="ada",
    )(cvec, w_ada, b_ada.reshape(L, 1, W))


def _inproj_kernel(*refs, n_stream, nct):
    x_refs = refs[:n_stream]
    (mod_ref, gpre_ref, w_ref, cosa_ref, sina_ref, cosc_ref, sinc_ref,
     gqk_ref, gcq_ref, gckv_ref, wuq_ref, wukv_ref,
     qa_ref, ka_ref, va_ref, qb_ref, kb_ref, vb_ref, misc_ref, gb_ref,
     qc_ref, kc_ref, vc_ref, gates_ref) = refs[n_stream:]
    x = _stream_tile(x_refs, nct)
    m = mod_ref[0, 0]
    h = (_rms(x, gpre_ref[...]) * (1.0 + m[1:2]) + m[0:1]).astype(BF16)

    def proj(lo, hi):
        return jnp.dot(h, w_ref[:, lo:hi], preferred_element_type=F32)


    zb = proj(_QB, _CQ)
    qb_ref[0] = zb[:, 0:256] * (B_KEY_DIM ** -0.5)
    kb_ref[0] = zb[:, 256:512]
    vb_ref[0] = zb[:, 512:1024].astype(BF16)
    cosc, sinc = cosc_ref[...], sinc_ref[...]
    misc = _rope_slab(zb[:, 1024:1152], cosc, sinc, C_ROPE // 2)
    misc_ref[0] = misc.astype(BF16)
    gb_ref[0] = zb[:, 1152:1664]

    zc = proj(_CQ, _GATES)
    cqn = _rms(zc[:, 0:C_Q_RANK], gcq_ref[...]).astype(BF16)
    qup = jnp.dot(cqn, wuq_ref[...], preferred_element_type=F32)
    scale_c = (C_NOPE + C_ROPE) ** -0.5 * LOG2E
    for hh in range(C_HEADS):
        sl = slice(hh * C_HEAD_PAD, (hh + 1) * C_HEAD_PAD)
        qc_ref[0, :, sl] = (_rope_slab(qup[:, sl], cosc, sinc, C_ROPE // 2) * scale_c).astype(BF16)
    ckvn = _rms(zc[:, C_Q_RANK:C_Q_RANK + C_KV_RANK], gckv_ref[...]).astype(BF16)
    kvup = jnp.dot(ckvn, wukv_ref[...], preferred_element_type=F32)
    lane = lax.broadcasted_iota(jnp.int32, misc.shape, 1)
    krope = jnp.where((lane >= _MISC_ROPE_LO) & (lane < _MISC_ROPE_LO + C_ROPE), misc, 0.0)
    for hh in range(C_HEADS):
        sl = slice(hh * C_HEAD_PAD, (hh + 1) * C_HEAD_PAD)
        kc_ref[0, :, sl] = (kvup[:, sl] + krope).astype(BF16)
    vc_ref[0] = kvup[:, C_HEADS * C_HEAD_PAD:].astype(BF16)

    za = proj(_QA, _QB)

    zg = proj(_GATES, _W_IN_P)
    gates_ref[0] = (1.0 / (1.0 + jnp.exp(-zg))).astype(BF16)

    cosa, sina = cosa_ref[...], sina_ref[...]

    def headnorm_rope(z, gain, out_ref, out_lo):
        w = z.shape[1]
        for s in range(w // LANES):
            zs = z[:, s * LANES:(s + 1) * LANES]
            ss = zs * zs
            half = A_HEAD_DIM // 2
            while half >= 1:
                ss = ss + _swap_halves(ss, half)
                half //= 2
            y = zs * lax.rsqrt(ss * (1.0 / A_HEAD_DIM) + EPS) * gain[:, s * LANES:(s + 1) * LANES]
            ys = _rope_slab(y, cosa, sina, A_HEAD_DIM // 2)
            out_ref[0, :, out_lo + s * LANES:out_lo + (s + 1) * LANES] = ys.astype(out_ref.dtype)

    headnorm_rope(za[:, 0:256], gqk_ref[:, 0:256], qa_ref, 0)
    headnorm_rope(za[:, 256:512], gqk_ref[:, 256:512], qa_ref, 256)
    headnorm_rope(za[:, 512:640], gqk_ref[:, 512:640], ka_ref, 0)
    va_ref[0] = za[:, 640:768].astype(BF16)


def _inproj(stream, modsel, gpre, w_in_p, tabs, gqk, gcq, gckv, wuq, wukv, n_ctx, layer):
    B, D = stream[-1].shape[0], stream[-1].shape[2]
    N = tabs[0].shape[0]
    tr = ROW_TILE
    nct = n_ctx // tr
    row =lambda w: pl.BlockSpec((1, tr, w), lambda b, t: (b, t, 0))
    tab = pl.BlockSpec((tr, LANES), lambda b, t: (t, 0))
    outs = [("qa", 512, BF16), ("ka", 128, BF16), ("va", 128, BF16),
            ("qb", 256, F32), ("kb", 256, F32), ("vb", 512, BF16), ("misc", 128, BF16), ("gb", 512, F32),
            ("qc", C_HEADS * C_HEAD_PAD, BF16), ("kc", C_HEADS * C_HEAD_PAD, BF16),
            ("vc", C_HEADS * C_VDIM, BF16), ("gates", N_BRANCH * D, BF16)]
    return pl.pallas_call(
        functools.partial(_inproj_kernel, n_stream=len(stream), nct=nct),
        grid=(B, N // tr),
        in_specs=_stream_specs(stream, tr, n_ctx, N) + [
            pl.BlockSpec((1, 1, N_MOD, D), lambda b, t: (b, jnp.where(t < nct, 0, 1), 0, 0)),
            _const_spec(gpre.shape), _layer_spec(w_in_p, layer),
            tab, tab, tab, tab,
            _const_spec(gqk.shape), _const_spec(gcq.shape),
            _const_spec(gckv.shape), _layer_spec(wuq, layer), _layer_spec(wukv, layer)],
        out_specs=[row(w) for _, w, _ in outs],
        out_shape=[jax.ShapeDtypeStruct((B, N, w), dt) for _, w, dt in outs],
        compiler_params=_cparams(2),
        name="inproj",
    )(*stream, modsel, gpre, w_in_p, *tabs, gqk, gcq, gckv, wuq, wukv)


def _attn_kernel(*refs, mixers, ahead):
    n_mix = len(mixers)
    jobs, v_aug = [], {}
    for i, (n_heads, q_w, kv_group, v_w) in enumerate(mixers):
        q_ref, k_ref, v_ref = refs[3 * i:3 * i + 3]
        ones = jnp.ones((k_ref.shape[1], LANES - v_w), BF16)
        for g in range(n_heads // kv_group):
            v_aug[i, g] = jnp.concatenate([v_ref[0, :, g * v_w:(g + 1) * v_w], ones], axis=1)
        jobs += [(i, hh, q_ref, k_ref, refs[3 * n_mix + i], q_w, kv_group, v_w) for hh in range(n_heads)]

    def scores(job):
        _, hh, q_ref, k_ref, _, q_w, kv_group, _ = job
        g = hh // kv_group
        return lax.dot_general(q_ref[0, :, hh * q_w:(hh + 1) * q_w], k_ref[0, :, g * q_w:(g + 1) * q_w],
                               (((1,), (1,)), ((), ())), preferred_element_type=F32)

    ahead = min(ahead, len(jobs))
    queue = [scores(job) for job in jobs[:ahead]]
    for n, job in enumerate(jobs):
        i, hh, _, _, o_ref, _, kv_group, v_w = job
        s = queue.pop(0)
        if n + ahead < len(jobs):
            queue.append(scores(jobs[n + ahead]))
        p = jnp.exp2(s - jnp.max(s, axis=-1, keepdims=True))
        o = jnp.dot(p.astype(BF16), v_aug[i, hh // kv_group], preferred_element_type=F32)
        o_ref[0, :, hh * v_w:(hh + 1) * v_w] = (o[:, 0:v_w] / o[:, v_w:v_w + 1]).astype(o_ref.dtype)


def _attention(qkv, mixers, n_ctx, need_ctx):
    B, N, _ = qkv[0][0].shape
    n_lat = N - n_ctx
    ows = [n_heads * v_w for n_heads, _, _, v_w in mixers]
    n_jobs = sum(n_heads for n_heads, _, _, _ in mixers)
    kern = functools.partial(_attn_kernel, mixers=tuple(mixers), ahead=1)
    kern_ctx = functools.partial(_attn_kernel, mixers=tuple(mixers), ahead=n_jobs)
    flat =[a for trio in qkv for a in trio]
    keys = lambda a, n: pl.BlockSpec((1, n, a.shape[2]), lambda b, t: (b, 0, 0))
    tq = ATTN_LATENT_TILE
    q_rows = lambda q: pl.BlockSpec((pl.Element(1), pl.Element(tq), pl.Element(q.shape[2])),
                                    lambda b, t: (b, pl.multiple_of(n_ctx + t * tq, ROW_TILE), 0))
    o_lat = pl.pallas_call(
        kern,
        grid=(B, n_lat // tq),
        in_specs=[s for q, k, v in qkv for s in (q_rows(q), keys(k, N), keys(v, N))],
        out_specs=[pl.BlockSpec((1, tq, ow), lambda b, t: (b, t, 0)) for ow in ows],
        out_shape=[jax.ShapeDtypeStruct((B, n_lat, ow), BF16) for ow in ows],
        compiler_params=_cparams(2),
        name="attn",
    )(*flat)
    if not need_ctx:
        return [(o,) for o in o_lat]
    tq = ROW_TILE
    rows = lambda w: pl.BlockSpec((1, tq, w), lambda b, t: (b, t, 0))
    o_ctx = pl.pallas_call(
        kern_ctx,
        grid=(B, n_ctx // tq),
        in_specs=[s for q, k, v in qkv for s in (rows(q.shape[2]), keys(k, n_ctx), keys(v, n_ctx))],
        out_specs=[rows(ow) for ow in ows],
        out_shape=[jax.ShapeDtypeStruct((B, n_ctx, ow), BF16) for ow in ows],
        compiler_params=_cparams(2),
        name="attn_ctx",
    )(*flat)
    return list(zip(o_ctx, o_lat))


def _gla_kernel(q_ref, k_ref, v_ref, misc_ref, gate_ref, wal_ref, bal_ref, gout_ref, o_ref,
                la_scr, acc_scr, s_scr, *, n_ctx):
    n_rows = q_ref.shape[1]
    ck, blk = B_CHUNK, GLA_BLOCK
    cpb = blk // ck
    nb, ncb = n_rows // blk, n_ctx // blk
    kw = B_HEADS * B_KEY_DIM

    def decay_rows(i, carry):
        r0 = pl.multiple_of(i * blk, blk)
        z = jnp.dot(misc_ref[0, pl.ds(r0, blk), :], wal_ref[...], preferred_element_type=F32) + bal_ref[...]
        la_scr[pl.ds(r0, blk), :] = (jnp.minimum(z, 0.0) - jnp.log1p(jnp.exp(-jnp.abs(z)))) * (1.0 / B_GATE_NORM)
        return carry
    lax.fori_loop(0, nb, decay_rows, 0)

    ri = lax.broadcasted_iota(jnp.int32, (blk, blk), 0)
    ci = lax.broadcasted_iota(jnp.int32, (blk, blk), 1)
    same_chunk = (ri // ck) == (ci // ck)

    heads = range(B_HEADS)
    ks = [slice(hh * B_KEY_DIM, (hh + 1) * B_KEY_DIM) for hh in heads]
    vs = [slice(hh * B_VAL_DIM, (hh + 1) * B_VAL_DIM) for hh in heads]

    def front(ib, backward):
        rows = pl.ds(pl.multiple_of(ib * blk, blk), blk)
        keep = same_chunk & ((ci >= ri) if backward else (ci <= ri))
        la = la_scr[rows, kw:2 * kw] if backward else la_scr[rows, 0:kw]
        tri = jnp.where(keep, 1.0, 0.0).astype(BF16)
        la_hi = la.astype(BF16)
        rem = la - la_hi.astype(F32)
        la_mid = rem.astype(BF16)
        la_lo = (rem - la_mid.astype(F32)).astype(BF16)
        b = (jnp.dot(tri, la_hi, preferred_element_type=F32)
             + jnp.dot(tri, la_mid, preferred_element_type=F32)
             + jnp.dot(tri, la_lo, preferred_element_type=F32))
        last = 0 if backward else ck - 1
        bl = jnp.concatenate([jnp.broadcast_to(b[j * ck + last:j * ck + last + 1, :], (ck, kw))
                              for j in range(cpb)], axis=0)
        qd = q_ref[0, rows, :] * jnp.exp(b)
        k = k_ref[0, rows, :]
        kd = k * jnp.exp(-b)
        ke_t = (k * jnp.exp(bl - b)).T
        g_t = jnp.exp(bl.T)
        v = v_ref[0, rows, :]
        attn = [lax.dot_general(qd[:, ks[hh]].astype(BF16), kd[:, ks[hh]].astype(BF16),
                                (((1,), (1,)), ((), ())), preferred_element_type=F32) for hh in heads]
        u_all = [jnp.dot(jnp.where(same_chunk, jnp.concatenate([ke_t[ks[hh], :]] * cpb, axis=0), 0.0).astype(BF16),
                         v[:, vs[hh]], preferred_element_type=F32) for hh in heads]
        o = [jnp.dot(jnp.where(keep, attn[hh], 0.0).astype(BF16), v[:, vs[hh]], preferred_element_type=F32)
             for hh in heads]
        qd4 = [jnp.where(same_chunk, jnp.concatenate([qd[:, ks[hh]]] * cpb, axis=1), 0.0).astype(BF16)
               for hh in heads]
        return g_t, u_all, o, qd4

    def back(fr, ib, backward):
        g_t, u_all, o, qd4 = fr
        rows = pl.ds(pl.multiple_of(ib * blk, blk), blk)
        order = range(cpb - 1, -1, -1) if backward else range(cpb)
        d = int(backward)
        s_stack = []
        for hh in heads:
            s = s_scr[d, hh]
            s_enter = [None] * cpb
            for j in order:
                s_enter[j] = s.astype(BF16)
                s = g_t[ks[hh], j * ck:j * ck + 1] * s + u_all[hh][j * ck:(j + 1) * ck, :]
            s_scr[d, hh] = s
            s_stack.append(jnp.concatenate(s_enter, axis=0))
        for hh in heads:
            tot = o[hh] + jnp.dot(qd4[hh], s_stack[hh], preferred_element_type=F32)
            if not backward:
                acc_scr[rows, vs[hh]] = tot
            else:
                tot = _rms(acc_scr[rows, vs[hh]] + tot, gout_ref[...])
                gt = gate_ref[0, rows, vs[hh]]
                o_ref[0, rows, vs[hh]] = (tot * (gt / (1.0 + jnp.exp(-gt)))).astype(o_ref.dtype)

    s_scr[...] = jnp.zeros_like(s_scr)
    def fwd(i, carry):
        back(front(i, False), i, False)
        return carry
    lax.fori_loop(0, nb, fwd, 0)

    def bwd(i, carry):
        ib = jnp.where(i < ncb, ncb - 1 - i, nb - 1 - (i - ncb))
        back(front(ib, True), ib, True)
        return carry
    lax.fori_loop(0, nb, bwd, 0)


def _gla(qb, kb, vb, misc, gb, wal, bal, gout, n_ctx):
    B, N, _ = qb.shape
    vw = B_HEADS * B_VAL_DIM
    blk = lambda w: pl.BlockSpec((1, N, w), lambda b: (b, 0, 0))
    return pl.pallas_call(
        functools.partial(_gla_kernel, n_ctx=n_ctx),
        grid=(B,),
        in_specs=[blk(qb.shape[2]), blk(kb.shape[2]), blk(vw), blk(LANES), blk(vw),
                  _const_spec(wal.shape), _const_spec(bal.shape), _const_spec(gout.shape)],
        out_specs=blk(vw),
        out_shape=jax.ShapeDtypeStruct((B, N, vw), BF16),
        scratch_shapes=[pltpu.VMEM((N, 2 * B_HEADS * B_KEY_DIM), F32),
                        pltpu.VMEM((N, vw), F32),
                        pltpu.VMEM((2, B_HEADS, B_KEY_DIM, B_VAL_DIM), F32)],
        compiler_params=_cparams(1),
        name="gla",
    )(qb, kb, vb, misc, gb, wal, bal, gout)


def _post_kernel(*refs, n_stream, n_attn, nct, t0):
    x_refs, refs = refs[:n_stream], refs[n_stream:]
    mod_ref, refs = refs[0], refs[1:]
    oa_refs, ob_ref, oc_refs = refs[:n_attn], refs[n_attn], refs[n_attn + 1:2 * n_attn + 1]
    (gates_ref, wbr_ref, wout_ref, gpost_ref, gpre2_ref,
     w1_ref, w2_ref, gpost2_ref, o_ref) = refs[2 * n_attn + 1:]
    d = o_ref.shape[2]
    m = mod_ref[0, 0]
    tr = o_ref.shape[1]
    halves = [slice(i * tr // POST_SPLIT, (i + 1) * tr // POST_SPLIT) for i in range(POST_SPLIT)]
    branches = (_stream_tile(oa_refs, nct, t0), ob_ref[0], _stream_tile(oc_refs, nct, t0))
    x = _stream_tile(x_refs, nct, t0)
    acc = []
    for r in halves:
        tot = None
        for z, branch in enumerate(branches):
            proj = jnp.dot(branch[r], wbr_ref[z], preferred_element_type=F32)
            term = gates_ref[0, r, z * d:(z + 1) * d].astype(F32) * proj
            tot = term if tot is None else tot + term
        acc.append(tot.astype(BF16))
    y = [jnp.dot(a, wout_ref[...], preferred_element_type=F32) for a in acc]
    x1 = [x[r] + m[2:3] * _rms(yh, gpost_ref[...]) for r, yh in zip(halves, y)]
    h2 = [(_rms(xh, gpre2_ref[...]) * (1.0 + m[4:5]) + m[3:4]).astype(BF16) for xh in x1]
    u = [jnp.dot(h, w1_ref[...], preferred_element_type=F32) for h in h2]
    a = [jnp.square(jnp.maximum(uh, 0.0)).astype(BF16) for uh in u]
    f = [jnp.dot(ah, w2_ref[...], preferred_element_type=F32) for ah in a]
    for r, xh, fh in zip(halves, x1, f):
        o_ref[0, r, :] = xh + m[5:6] * _rms(fh, gpost2_ref[...])


def _post(stream, modsel, oa, ob, oc, gates, wbr, wout, gpost, gpre2, w1, w2, gpost2, n_ctx, need_ctx, layer):
    B, N, _ = ob.shape
    D = stream[-1].shape[2]
    tr = ROW_TILE
    nct = n_ctx // tr
    t0 = 0 if need_ctx else nct
    specs = lambda arrs: _stream_specs(arrs, tr, n_ctx, N, t0)
    mod = pl.BlockSpec((1, 1, N_MOD, D), lambda b, t: (b, jnp.where(t + t0 < nct, 0, 1), 0, 0))
    consts = (wbr, wout, gpost, gpre2, w1, w2, gpost2)
    const_specs = [_layer_spec(w, layer) if w.ndim > 2 else _const_spec(w.shape) for w in consts]
    assert len(oa) == len(oc)
    return pl.pallas_call(
        functools.partial(_post_kernel, n_stream=len(stream), n_attn=len(oa), nct=nct, t0=t0),
        grid=(B, N // tr - t0),
        in_specs=specs(stream) + [mod] + specs(oa) + specs((ob,)) + specs(oc) + specs((gates,)) + const_specs,
        out_specs=pl.BlockSpec((1, tr, D), lambda b, t: (b, t, 0)),
        out_shape=jax.ShapeDtypeStruct((B, N - t0 * tr, D), F32),
        compiler_params=_cparams(2),
        name="post",
    )(*stream, modsel, *oa, ob, *oc, gates, *consts)


def _pairs_split(w, n_heads):
    lead, hd = w.shape[:-1], w.shape[-1] // n_heads
    return w.reshape(lead + (n_heads, hd // 2, 2)).swapaxes(-1, -2).reshape(lead + (n_heads * hd,))


def _w_in_layout(w_in):
    L, D, _ = w_in.shape
    w_in = w_in.astype(BF16)
    z32 = jnp.zeros((L, D, 32), w_in.dtype)
    misc = jnp.concatenate([w_in[..., 1792:1824], z32, _pairs_split(w_in[..., 2976:3008], 1), z32], axis=-1)
    parts = [_pairs_split(w_in[..., 0:512], A_HEADS), _pairs_split(w_in[..., 512:640], A_KV_HEADS),
             w_in[..., 640:1792], misc, w_in[..., 1824:2976], w_in[..., 3008:]]
    out = jnp.concatenate(parts, axis=-1).astype(BF16)
    assert out.shape[-1] == _W_IN_P
    return out


def _w_uq_layout(w_uq):
    L, R, _ = w_uq.shape
    w = w_uq.reshape(L, R, C_HEADS, C_NOPE + C_ROPE)
    rope = _pairs_split(w[..., C_NOPE:], 1)
    pad = jnp.zeros((L, R, C_HEADS, C_HEAD_PAD - C_NOPE - C_ROPE), w.dtype)
    return jnp.concatenate([w[..., :C_NOPE], rope, pad], axis=-1).reshape(L, R, C_HEADS * C_HEAD_PAD).astype(BF16)


def _w_ukv_layout(w_ukv):
    L, R, _ = w_ukv.shape
    w = w_ukv.reshape(L, R, C_HEADS, C_NOPE + C_VDIM)
    pad = jnp.zeros((L, R, C_HEADS, C_HEAD_PAD - C_NOPE), w.dtype)
    k = jnp.concatenate([w[..., :C_NOPE], pad], axis=-1).reshape(L, R, C_HEADS * C_HEAD_PAD)
    v = w[..., C_NOPE:].reshape(L, R, C_HEADS * C_VDIM)
    return jnp.concatenate([k, v], axis=-1).astype(BF16)


def _w_alpha_layout(w_alpha):
    L, _, r, kw = w_alpha.shape
    z = jnp.zeros((L, r, kw), w_alpha.dtype)
    top = jnp.concatenate([w_alpha[:, 0], z], axis=-1)
    bot = jnp.concatenate([z, w_alpha[:, 1]], axis=-1)
    return jnp.concatenate([top, bot, jnp.zeros((L, LANES - 2 * r, 2 * kw), w_alpha.dtype)], axis=1).astype(BF16)


def _rope_angles(n_lat, rot_dim):
    rows = n_lat // GRID_W
    row = jnp.repeat(jnp.arange(rows, dtype=F32), GRID_W)
    col = jnp.tile(jnp.arange(GRID_W, dtype=F32), rows)
    n_freq = rot_dim // 4
    inv_freq = ROPE_THETA ** (-jnp.arange(n_freq, dtype=F32) / n_freq)
    ang = jnp.concatenate([row[:, None] * inv_freq, col[:, None] * inv_freq], axis=-1)
    return jnp.cos(ang), jnp.sin(ang)


def _rope_tables(n_ctx, n_lat):
    cos, sin = _rope_angles(n_lat, A_HEAD_DIM)
    cos_a = jnp.concatenate([cos] * 4, axis=-1)
    sin_a = jnp.concatenate([-sin, sin, -sin, sin], axis=-1)
    cos, sin = _rope_angles(n_lat, C_ROPE)
    lo, hi = _MISC_ROPE_LO, LANES - _MISC_ROPE_LO - C_ROPE
    cos_c = jnp.concatenate([jnp.ones((n_lat, lo), F32), cos, cos, jnp.ones((n_lat, hi), F32)], axis=-1)
    sin_c = jnp.concatenate([jnp.zeros((n_lat, lo), F32), -sin, sin, jnp.zeros((n_lat, hi), F32)], axis=-1)
    ident = lambda t, v: jnp.concatenate([jnp.full((n_ctx, LANES), v, F32), t], axis=0)
    return ident(cos_a, 1.0), ident(sin_a, 0.0), ident(cos_c, 1.0), ident(sin_c, 0.0)


def kernel(x, c, ctx, c_ctx, w_ada, b_ada, g_pre_attn, g_post_attn, g_pre_mlp, g_post_mlp, w_in, a_g_q, a_g_k,
           b_w_alpha, b_b_alpha, b_g_out, c_g_q, c_g_kv, c_w_uq, c_w_ukv, w_branch, w_out, w_mlp_in, w_mlp_out):
    B, n_lat, D = x.shape
    n_ctx = ctx.shape[1]
    L = w_ada.shape[0]
    assert n_ctx % ROW_TILE == 0 and n_lat % ROW_TILE == 0
    assert n_ctx % GLA_BLOCK == 0 and n_lat % GLA_BLOCK == 0 and GLA_BLOCK % B_CHUNK == 0
    assert n_lat % ATTN_LATENT_TILE == 0

    n_vec = -(-(B + 1) // 8) * 8
    cvec = jnp.concatenate([c, c_ctx[None], jnp.zeros((n_vec - B - 1, D), F32)], axis=0)
    mods = _ada(cvec, w_ada, b_ada)
    mod_lat = mods[:, :B].reshape(L, B, 1, N_MOD, D)
    mod_ctx = jnp.broadcast_to(mods[:, B].reshape(L, 1, 1, N_MOD, D), (L, B, 1, N_MOD, D))
    modsel = jnp.concatenate([mod_ctx, mod_lat], axis=2)

    tabs = _rope_tables(n_ctx, n_lat)
    w_in_p = _w_in_layout(w_in)
    gqk = jnp.concatenate([jnp.tile(_pairs_split(a_g_q, 1) * (A_HEAD_DIM ** -0.5 * LOG2E), (1, A_HEADS)),
                           jnp.tile(_pairs_split(a_g_k, 1), (1, A_KV_HEADS))], axis=-1)
    wuq, wukv = _w_uq_layout(c_w_uq), _w_ukv_layout(c_w_ukv)
    wal = _w_alpha_layout(b_w_alpha)
    bal = b_b_alpha.reshape(L, 1, 2 * B_HEADS * B_KEY_DIM)
    wbr, wout = w_branch.astype(BF16), w_out.astype(BF16)
    w1, w2 = w_mlp_in.astype(BF16), w_mlp_out.astype(BF16)
    vec = lambda g, l: g[l][None, :]

    stream = (ctx, x)
    for l in range(L):
        need_ctx = l < L - 1
        qa, ka, va, qb, kb, vb, misc, gb, qc, kc, vc, gates = _inproj(
            stream, modsel[l], vec(g_pre_attn, l), w_in_p, tabs, vec(gqk, l),
            vec(c_g_q, l), vec(c_g_kv, l), wuq, wukv, n_ctx, l)
        oa, oc = _attention([(qa, ka, va), (qc, kc, vc)],
                            [(A_HEADS, A_HEAD_DIM, A_GROUP, A_HEAD_DIM), (C_HEADS, C_HEAD_PAD, 1, C_VDIM)],
                            n_ctx, need_ctx)
        ob = _gla(qb, kb, vb, misc, gb, wal[l], bal[l], vec(b_g_out, l), n_ctx)
        stream = (_post(stream, modsel[l], oa, ob, oc, gates, wbr, wout, vec(g_post_attn, l),
                        vec(g_pre_mlp, l), w1, w2, vec(g_post_mlp, l), n_ctx, need_ctx, l),)
    return stream[0]
```
